```python
import math
import jax, jax.numpy as jnp
from jax import lax
import numpy as np

D_MODEL = 1024
BATCH = 4
SEQ = 4096
DEPTH = 2
DEC_BATCH = 2
DEC_SEQ = 8192
PAST_LEN = 128

N_META = 16
GRID_W = 64
Q_BLOCK = 128
D_MIX = D_MODEL
DA_HEADS = 4
DA_DIM = 64
DA_WIDTH = DA_HEADS * 2 * DA_DIM
GB_HEADS = 8
GB_KV = 2
GB_DIM = 64
GB_WIDTH = GB_HEADS * GB_DIM
GB_KV_WIDTH = GB_KV * GB_DIM
ROPE_THETA = 10000.0
D_IN = 3 * DA_WIDTH + GB_WIDTH + 2 * GB_KV_WIDTH
SPLIT_POINTS = (DA_WIDTH, 2 * DA_WIDTH, 3 * DA_WIDTH, 3 * DA_WIDTH + GB_WIDTH, 3 * DA_WIDTH + GB_WIDTH + GB_KV_WIDTH)
D_FF = 2816
N_EXPERTS = 8
TOP_K = 2
D_FF_E = 3584
N_DENSE = (DEPTH + 1) // 2
N_MOE = DEPTH // 2
EPS = 1e-6

kernel_name = "hymba_diffattn_gqa2drope_moe_encoder"


def rms_norm(x, g):
    xf = x.astype(jnp.float32)
    y = xf * lax.rsqrt(jnp.mean(xf * xf, axis=-1, keepdims=True) + EPS)
    return (y * g.astype(jnp.float32)).astype(x.dtype)


def alibi_slopes():
    start = 2.0 ** (-8.0 / DA_HEADS)
    return start ** jnp.arange(1, DA_HEADS + 1, dtype=jnp.float32)


def token_grid(n_tok):
    rows = n_tok // GRID_W
    row = jnp.repeat(jnp.arange(rows, dtype=jnp.float32), GRID_W)
    col = jnp.tile(jnp.arange(GRID_W, dtype=jnp.float32), rows)
    meta = jnp.arange(N_META, dtype=jnp.float32) - N_META
    return jnp.concatenate([meta, row]), jnp.concatenate([meta, col])


def axial_rope_tables(row, col):
    n_freq = GB_DIM // 4
    inv = ROPE_THETA ** (-jnp.arange(n_freq, dtype=jnp.float32) * 2.0 / (GB_DIM // 2))
    ang = jnp.concatenate([row[:, None] * inv, col[:, None] * inv], axis=-1)
    return jnp.cos(ang), jnp.sin(ang)


def apply_rope(x, cos, sin):
    xf = x.astype(jnp.float32).reshape(x.shape[:-1] + (GB_DIM // 2, 2))
    x0, x1 = xf[..., 0], xf[..., 1]
    c = cos[None, :, None, :]
    s = sin[None, :, None, :]
    out = jnp.stack([x0 * c - x1 * s, x0 * s + x1 * c], axis=-1).reshape(x.shape)
    return out.astype(x.dtype)


def sweep_query_blocks(block_fn, q, q_pos):
    meta_out = block_fn(q[:, :N_META], q_pos[:N_META])
    n_tok = q.shape[1] - N_META
    n_blk = n_tok // Q_BLOCK
    qb = q[:, N_META:].reshape((q.shape[0], n_blk, Q_BLOCK) + q.shape[2:])
    qb = jnp.moveaxis(qb, 1, 0)
    pos_blk = q_pos[N_META:].reshape(n_blk, Q_BLOCK)
    outs = lax.map(lambda a: block_fn(a[0], a[1]), (qb, pos_blk))
    outs = jnp.moveaxis(outs, 0, 1).reshape((outs.shape[1], n_tok) + outs.shape[3:])
    return jnp.concatenate([meta_out, outs], axis=1)


def diff_attention(q, k, v, lam, slopes):
    L = k.shape[1]
    k_pos = jnp.arange(L, dtype=jnp.float32)
    k1, k2 = k[..., 0, :], k[..., 1, :]
    scale = DA_DIM ** -0.5

    def block(qb, qpos):
        dist = jnp.abs(qpos.astype(jnp.float32)[:, None] - k_pos[None, :])
        bias = -slopes[:, None, None] * dist[None]
        s1 = jnp.einsum('bqhd,bkhd->bhqk', qb[..., 0, :], k1).astype(jnp.float32) * scale + bias
        s2 = jnp.einsum('bqhd,bkhd->bhqk', qb[..., 1, :], k2).astype(jnp.float32) * scale + bias
        a = jax.nn.softmax(s1, axis=-1) - lam * jax.nn.softmax(s2, axis=-1)
        return jnp.einsum('bhqk,bkhd->bqhd', a.astype(v.dtype), v)

    return sweep_query_blocks(block, q, jnp.arange(L, dtype=jnp.int32))


def gqa_attention(q, k, v):
    B, L = q.shape[0], q.shape[1]
    G = GB_HEADS // GB_KV
    scale = GB_DIM ** -0.5

    def block(qb, qpos):
        nq = qb.shape[1]
        qg = qb.reshape(B, nq, GB_KV, G, GB_DIM)
        s = jnp.einsum('bqkgd,bskd->bkgqs', qg, k).astype(jnp.float32) * scale
        p = jax.nn.softmax(s, axis=-1).astype(v.dtype)
        o = jnp.einsum('bkgqs,bskd->bqkgd', p, v)
        return o.reshape(B, nq, GB_HEADS, GB_DIM)

    return sweep_query_blocks(block, q, jnp.arange(L, dtype=jnp.int32))


def mixer_layer(h, norm_g, w_in, lq1, lk1, lq2, lk2, subln_g, qn_g, kn_g, w_out, lambda_init, cos, sin, slopes):
    B, L, _ = h.shape
    u = rms_norm(h, norm_g)
    proj = u @ w_in
    a_q, a_k, a_v, b_q, b_k, b_v = jnp.split(proj, SPLIT_POINTS, axis=-1)
    a_q = a_q.reshape(B, L, DA_HEADS, 2, DA_DIM)
    a_k = a_k.reshape(B, L, DA_HEADS, 2, DA_DIM)
    a_v = a_v.reshape(B, L, DA_HEADS, 2 * DA_DIM)
    f32 = jnp.float32
    lam = (jnp.exp(jnp.sum(lq1.astype(f32) * lk1.astype(f32)))
           - jnp.exp(jnp.sum(lq2.astype(f32) * lk2.astype(f32))) + lambda_init)
    oa = diff_attention(a_q, a_k, a_v, lam, slopes)
    oa = rms_norm(oa, subln_g) * (1.0 - lambda_init)
    b_q = apply_rope(rms_norm(b_q.reshape(B, L, GB_HEADS, GB_DIM), qn_g), cos, sin)
    b_k = apply_rope(rms_norm(b_k.reshape(B, L, GB_KV, GB_DIM), kn_g), cos, sin)
    b_v = b_v.reshape(B, L, GB_KV, GB_DIM)
    ob = gqa_attention(b_q, b_k, b_v)
    o = jnp.concatenate([oa.reshape(B, L, DA_WIDTH), ob.reshape(B, L, GB_WIDTH)], axis=-1)
    return h + o @ w_out


def swiglu(x, wg, wu, wd):
    return (jax.nn.silu(x @ wg) * (x @ wu)) @ wd


def moe_swiglu(x, w_router, wg, wu, wd):
    B, L, D = x.shape
    xt = x.reshape(B * L, D)
    logits = (xt @ w_router).astype(jnp.float32)
    top_v, top_i = lax.top_k(logits, TOP_K)
    gates = jax.nn.softmax(top_v, axis=-1)
    combine = jnp.sum(jax.nn.one_hot(top_i, N_EXPERTS, dtype=jnp.float32) * gates[..., None], axis=1)
    y = jnp.zeros_like(xt)
    for e in range(N_EXPERTS):
        y = y + combine[:, e:e + 1].astype(x.dtype) * swiglu(xt, wg[e], wu[e], wd[e])
    return y.reshape(B, L, D)


def trunk(x, meta_tokens, norm1_g, w_in, lambda_q1, lambda_k1, lambda_q2, lambda_k2, subln_g,
          q_norm_g, k_norm_g, w_out, norm2_g, ffn_w_gate, ffn_w_up, ffn_w_down,
          router_w, moe_w_gate, moe_w_up, moe_w_down, final_norm_g):
    B, S, D = x.shape
    meta = jnp.broadcast_to(meta_tokens[None].astype(x.dtype), (B, N_META, D))
    h = jnp.concatenate([meta, x], axis=1)
    row, col = token_grid(S)
    cos, sin = axial_rope_tables(row, col)
    slopes = alibi_slopes()
    for layer in range(DEPTH):
        lambda_init = 0.8 - 0.6 * math.exp(-0.3 * layer)
        h = mixer_layer(h, norm1_g[layer], w_in[layer], lambda_q1[layer], lambda_k1[layer],
                        lambda_q2[layer], lambda_k2[layer], subln_g[layer], q_norm_g[layer],
                        k_norm_g[layer], w_out[layer], lambda_init, cos, sin, slopes)
        u = rms_norm(h, norm2_g[layer])
        if layer % 2 == 0:
            i = layer // 2
            h = h + swiglu(u, ffn_w_gate[i], ffn_w_up[i], ffn_w_down[i])
        else:
            i = layer // 2
            h = h + moe_swiglu(u, router_w[i], moe_w_gate[i], moe_w_up[i], moe_w_down[i])
    h = rms_norm(h, final_norm_g)
    return h[:, N_META:]


def setup_inputs(seed: int = 0) -> dict:
    key = jax.random.key(seed)
    ks = jax.random.split(key, 24)
    f32 = jnp.float32

    def nrm(k, shape, scale):
        return jax.random.normal(k, shape, dtype=f32) * scale

    def gain(k, shape):
        return 1.0 + 0.01 * jax.random.normal(k, shape, dtype=f32)

    return {
        "x_prompt": nrm(ks[0], (BATCH, SEQ, D_MODEL), 1.0),
        "x_sample": nrm(ks[1], (DEC_BATCH, DEC_SEQ, D_MODEL), 1.0),
        "meta_tokens": nrm(ks[2], (N_META, D_MODEL), 1.0),
        "norm1_g": gain(ks[3], (DEPTH, D_MODEL)),
        "w_in": nrm(ks[4], (DEPTH, D_MODEL, D_IN), D_MODEL ** -0.5),
        "lambda_q1": nrm(ks[5], (DEPTH, DA_DIM), 0.1),
        "lambda_k1": nrm(ks[6], (DEPTH, DA_DIM), 0.1),
        "lambda_q2": nrm(ks[7], (DEPTH, DA_DIM), 0.1),
        "lambda_k2": nrm(ks[8], (DEPTH, DA_DIM), 0.1),
        "subln_g": gain(ks[9], (DEPTH, 2 * DA_DIM)),
        "q_norm_g": gain(ks[10], (DEPTH, GB_DIM)),
        "k_norm_g": gain(ks[11], (DEPTH, GB_DIM)),
        "w_out": nrm(ks[12], (DEPTH, D_MIX, D_MODEL), D_MIX ** -0.5),
        "norm2_g": gain(ks[13], (DEPTH, D_MODEL)),
        "ffn_w_gate": nrm(ks[14], (N_DENSE, D_MODEL, D_FF), D_MODEL ** -0.5),
        "ffn_w_up": nrm(ks[15], (N_DENSE, D_MODEL, D_FF), D_MODEL ** -0.5),
        "ffn_w_down": nrm(ks[16], (N_DENSE, D_FF, D_MODEL), D_FF ** -0.5),
        "router_w": nrm(ks[17], (N_MOE, D_MODEL, N_EXPERTS), D_MODEL ** -0.5),
        "moe_w_gate": nrm(ks[18], (N_MOE, N_EXPERTS, D_MODEL, D_FF_E), D_MODEL ** -0.5),
        "moe_w_up": nrm(ks[19], (N_MOE, N_EXPERTS, D_MODEL, D_FF_E), D_MODEL ** -0.5),
        "moe_w_down": nrm(ks[20], (N_MOE, N_EXPERTS, D_FF_E, D_MODEL), D_FF_E ** -0.5),
        "final_norm_g": gain(ks[21], (D_MODEL,)),
    }


def reference(x_prompt, x_sample, meta_tokens, norm1_g, w_in, lambda_q1, lambda_k1, lambda_q2, lambda_k2,
              subln_g, q_norm_g, k_norm_g, w_out, norm2_g, ffn_w_gate, ffn_w_up, ffn_w_down,
              router_w, moe_w_gate, moe_w_up, moe_w_down, final_norm_g):
    y_prompt = trunk(x_prompt, meta_tokens, norm1_g, w_in, lambda_q1, lambda_k1, lambda_q2, lambda_k2,
                     subln_g, q_norm_g, k_norm_g, w_out, norm2_g, ffn_w_gate, ffn_w_up, ffn_w_down,
                     router_w, moe_w_gate, moe_w_up, moe_w_down, final_norm_g)
    y_sample = trunk(x_sample, meta_tokens, norm1_g, w_in, lambda_q1, lambda_k1, lambda_q2, lambda_k2,
                     subln_g, q_norm_g, k_norm_g, w_out, norm2_g, ffn_w_gate, ffn_w_up, ffn_w_down,
                     router_w, moe_w_gate, moe_w_up, moe_w_down, final_norm_g)
    return (y_prompt, y_sample)
```

```python
import functools
import math

import jax
import jax.numpy as jnp
from jax import lax
from jax.experimental import pallas as pl
from jax.experimental.pallas import tpu as pltpu

F32 = jnp.float32
BF16 = jnp.bfloat16

N_META = 16
GRID_W = 64
DA_HEADS = 4
DA_DIM = 64
DA_WIDTH = DA_HEADS * 2 * DA_DIM
GB_HEADS = 8
GB_KV = 2
GB_GROUP = GB_HEADS // GB_KV
GB_DIM = 64
GB_WIDTH = GB_HEADS * GB_DIM
GB_KV_WIDTH = GB_KV * GB_DIM
ROPE_THETA = 10000.0
N_EXPERTS = 8
TOP_K = 2
EPS = 1e-6
QK_NORM_WIDTH = GB_WIDTH + GB_KV_WIDTH
LANES = 128
ROW_TILE = 256
ATTN_A_TQ = 256
ATTN_B_TQ = 128
ATTN_TK = 512
VMEM_LIMIT = 56 * 1024 * 1024
Q_SCALE = (DA_DIM ** -0.5) * math.log2(math.e)
NEG_INF = float("-inf")


def _cparams(sem):
    return pltpu.CompilerParams(dimension_semantics=sem, vmem_limit_bytes=VMEM_LIMIT)


def _const_spec(shape):
    nd = len(shape)
    return pl.BlockSpec(shape, lambda *_: (0,) * nd, pipeline_mode=pl.Buffered(1))


def _rms(x, g):
    return x * lax.rsqrt(jnp.mean(x * x, axis=-1, keepdims=True) + EPS) * g


def _dot(a, b):
    return jnp.dot(a, b, preferred_element_type=F32)


def _dot_nt(a, b):
    return lax.dot_general(a, b, (((1,), (1,)), ((), ())), preferred_element_type=F32)


def _in_proj_kernel(h_ref, g_ref, w_ref, cs_ref, sn_ref, gqk_ref, bd_ref,
                    aq_ref, ak_ref, av_ref, bq_ref, bk_ref, bv_ref):
    u = _rms(h_ref[...], g_ref[...]).astype(BF16)
    y = _dot(u, w_ref[...])
    aq_ref[...] = (y[:, :DA_WIDTH] * Q_SCALE).astype(BF16)
    ak_ref[...] = y[:, DA_WIDTH:2 * DA_WIDTH].astype(BF16)
    av_ref[...] = y[:, 2 * DA_WIDTH:3 * DA_WIDTH].astype(BF16)
    c0 = 3 * DA_WIDTH
    t = y[:, c0:c0 + QK_NORM_WIDTH]
    t2 = t * t
    hi = t2.astype(BF16)
    lo = (t2 - hi.astype(F32)).astype(BF16)
    bd = bd_ref[...]
    ss = _dot(hi, bd) + _dot(lo, bd)
    n = t * lax.rsqrt(ss * (1.0 / GB_DIM) + EPS) * gqk_ref[...]
    lane = lax.broadcasted_iota(jnp.int32, (t.shape[0], LANES), 1)
    first_half = (lane % GB_DIM) < (GB_DIM // 2)
    cs = cs_ref[...]
    sn = sn_ref[...]
    for c in range(QK_NORM_WIDTH // LANES):
        xc = n[:, c * LANES:(c + 1) * LANES]
        partner = jnp.where(first_half, pltpu.roll(xc, LANES - GB_DIM // 2, 1), pltpu.roll(xc, GB_DIM // 2, 1))
        r = xc * cs + partner * sn
        if c < GB_WIDTH // LANES:
            bq_ref[:, c * LANES:(c + 1) * LANES] = (r * Q_SCALE).astype(BF16)
        else:
            for kv in range(GB_KV):
                bk_ref[kv] = r[:, kv * GB_DIM:(kv + 1) * GB_DIM].astype(BF16)
    c1 = c0 + QK_NORM_WIDTH
    for kv in range(GB_KV):
        bv_ref[kv] = y[:, c1 + kv * GB_DIM:c1 + (kv + 1) * GB_DIM].astype(BF16)


def _in_proj(h, g, w, cs, sn, gqk, bd):
    rows, d = h.shape
    d_in = w.shape[1]
    tm = ROW_TILE
    row = lambda i: (i, 0)
    return pl.pallas_call(
        _in_proj_kernel,
        grid=(rows // tm,),
        in_specs=[
            pl.BlockSpec((tm, d), row),
            _const_spec((1, d)),
            _const_spec((d, d_in)),
            pl.BlockSpec((tm, LANES), row),
            pl.BlockSpec((tm, LANES), row),
            _const_spec((1, QK_NORM_WIDTH)),
            _const_spec((QK_NORM_WIDTH, QK_NORM_WIDTH)),
        ],
        out_specs=[
            pl.BlockSpec((tm, DA_WIDTH), row),
            pl.BlockSpec((tm, DA_WIDTH), row),
            pl.BlockSpec((tm, DA_WIDTH), row),
            pl.BlockSpec((tm, GB_WIDTH), row),
            pl.BlockSpec((GB_KV, tm, GB_DIM), lambda i: (0, i, 0)),
            pl.BlockSpec((GB_KV, tm, GB_DIM), lambda i: (0, i, 0)),
        ],
        out_shape=[
            jax.ShapeDtypeStruct((rows, DA_WIDTH), BF16),
            jax.ShapeDtypeStruct((rows, DA_WIDTH), BF16),
            jax.ShapeDtypeStruct((rows, DA_WIDTH), BF16),
            jax.ShapeDtypeStruct((rows, GB_WIDTH), BF16),
            jax.ShapeDtypeStruct((GB_KV, rows, GB_DIM), BF16),
            jax.ShapeDtypeStruct((GB_KV, rows, GB_DIM), BF16),
        ],
        compiler_params=_cparams(("parallel",)),
        name="in_proj",
    )(h, g, w, cs, sn, gqk, bd)


def _softmax_step(s, v, m_ref, l_ref, acc_ref, j):
    m_prev = m_ref[j]
    m_new = jnp.maximum(m_prev, jnp.max(s, axis=1, keepdims=True))
    alpha = jnp.exp2(m_prev - m_new)
    p = jnp.exp2(s - m_new)
    l_ref[j] = alpha * l_ref[j] + jnp.sum(p, axis=1, keepdims=True)
    acc_ref[j] = alpha * acc_ref[j] + _dot(p.astype(BF16), v)
    m_ref[j] = m_new


def _attn_a_kernel(nslope_ref, lam_ref, q_ref, k_ref, v_ref, mk_ref, mv_ref, sg_ref, oin_ref,
                   o_ref, m_ref, l_ref, acc_ref, *, tq, tk, nk, q_is_meta, out_scale):
    del oin_ref
    nslope = nslope_ref[pl.program_id(1)]
    q = q_ref[...]
    qs = (q[:, :DA_DIM], q[:, DA_DIM:])
    row = lax.broadcasted_iota(jnp.int32, (tq, 1), 0)
    if q_is_meta:
        qpos = row.astype(F32)
    else:
        qpos = (row + (pl.program_id(2) * tq + N_META)).astype(F32)
    m_ref[...] = jnp.full(m_ref.shape, NEG_INF, F32)
    l_ref[...] = jnp.zeros(l_ref.shape, F32)
    acc_ref[...] = jnp.zeros(acc_ref.shape, F32)

    def tile(kb, vb, kpos):
        bias = jnp.abs(qpos - kpos) * nslope
        for j in range(2):
            s = _dot_nt(qs[j], kb[:, j * DA_DIM:(j + 1) * DA_DIM]) + bias
            _softmax_step(s, vb, m_ref, l_ref, acc_ref, j)

    def body(i, carry):
        k0 = pl.multiple_of(i * tk, tk)
        col = lax.broadcasted_iota(jnp.int32, (1, tk), 1)
        tile(k_ref[pl.ds(k0, tk), :], v_ref[pl.ds(k0, tk), :], (col + (k0 + N_META)).astype(F32))
        return carry

    lax.fori_loop(0, nk, body, 0)
    tile(mk_ref[...], mv_ref[...], lax.broadcasted_iota(jnp.int32, (1, N_META), 1).astype(F32))
    o = acc_ref[0] / l_ref[0] - lam_ref[0] * (acc_ref[1] / l_ref[1])
    o_ref[...] = (_rms(o, sg_ref[...]) * out_scale).astype(o_ref.dtype)


def _attn_b_kernel(q_ref, k_ref, v_ref, mk_ref, mv_ref, oin_ref,
                   o_ref, qs_ref, m_ref, l_ref, acc_ref, *, tq, tk, nk):
    del oin_ref
    for g in range(GB_GROUP):
        qs_ref[g * tq:(g + 1) * tq, :] = q_ref[:, g * GB_DIM:(g + 1) * GB_DIM]
    qs = qs_ref[...]
    m_ref[...] = jnp.full(m_ref.shape, NEG_INF, F32)
    l_ref[...] = jnp.zeros(l_ref.shape, F32)
    acc_ref[...] = jnp.zeros(acc_ref.shape, F32)

    def tile(kb, vb):
        _softmax_step(_dot_nt(qs, kb), vb, m_ref, l_ref, acc_ref, 0)

    def body(i, carry):
        k0 = pl.multiple_of(i * tk, tk)
        tile(k_ref[pl.ds(k0, tk), :], v_ref[pl.ds(k0, tk), :])
        return carry

    lax.fori_loop(0, nk, body, 0)
    tile(mk_ref[...], mv_ref[...])
    o = acc_ref[0] / l_ref[0]
    for g in range(GB_GROUP):
        o_ref[:, g * GB_DIM:(g + 1) * GB_DIM] = o[g * tq:(g + 1) * tq].astype(o_ref.dtype)


def _attention(o_buf, aq, ak, av, bq, bk, bv, nslope, lam, subln_g, out_scale, *,
               n_seq, seq_len, row0, seq0, n_real, q_is_meta):
    assert row0 % seq_len == 0 and seq_len % ATTN_TK == 0
    nk = seq_len // ATTN_TK
    kv_blk0 = row0 // seq_len
    meta_blk0 = n_real // N_META + seq0
    any_spec = pl.BlockSpec(memory_space=pl.ANY)
    smem = pl.BlockSpec(memory_space=pltpu.SMEM)

    tq = N_META if q_is_meta else ATTN_A_TQ
    nq = 1 if q_is_meta else seq_len // tq
    if q_is_meta:
        q_map = lambda b, h, i: (meta_blk0 + b, h)
    else:
        q_blk0 = row0 // tq
        q_map = lambda b, h, i: (q_blk0 + b * nq + i, h)
    kv_map = lambda b, h, i: (kv_blk0 + b, h)
    mkv_map = lambda b, h, i: (meta_blk0 + b, h)
    w = 2 * DA_DIM
    o_buf = pl.pallas_call(
        functools.partial(_attn_a_kernel, tq=tq, tk=ATTN_TK, nk=nk, q_is_meta=q_is_meta, out_scale=out_scale),
        grid=(n_seq, DA_HEADS, nq),
        in_specs=[
            smem, smem,
            pl.BlockSpec((tq, w), q_map),
            pl.BlockSpec((seq_len, w), kv_map),
            pl.BlockSpec((seq_len, w), kv_map),
            pl.BlockSpec((N_META, w), mkv_map),
            pl.BlockSpec((N_META, w), mkv_map),
            pl.BlockSpec((1, w), lambda b, h, i: (0, 0)),
            any_spec,
        ],
        out_specs=pl.BlockSpec((tq, w), q_map),
        out_shape=jax.ShapeDtypeStruct(o_buf.shape, o_buf.dtype),
        scratch_shapes=[
            pltpu.VMEM((2, tq, 1), F32),
            pltpu.VMEM((2, tq, 1), F32),
            pltpu.VMEM((2, tq, w), F32),
        ],
        input_output_aliases={8: 0},
        compiler_params=_cparams(("parallel", "parallel", "parallel")),
        name="attn_diff_meta" if q_is_meta else "attn_diff",
    )(nslope, lam, aq, ak, av, ak, av, subln_g, o_buf)

    tq = N_META if q_is_meta else ATTN_B_TQ
    nq = 1 if q_is_meta else seq_len // tq
    qw = GB_GROUP * GB_DIM
    ocol0 = DA_WIDTH // qw
    if q_is_meta:
        q_map = lambda b, c, i: (meta_blk0 + b, c)
        o_map = lambda b, c, i: (meta_blk0 + b, ocol0 + c)
    else:
        q_blk0 = row0 // tq
        q_map = lambda b, c, i: (q_blk0 + b * nq + i, c)
        o_map = lambda b, c, i: (q_blk0 + b * nq + i, ocol0 + c)
    kv_map = lambda b, c, i: (c, kv_blk0 + b, 0)
    mkv_map = lambda b, c, i: (c, meta_blk0 + b, 0)
    o_buf = pl.pallas_call(
        functools.partial(_attn_b_kernel, tq=tq, tk=ATTN_TK, nk=nk),
        grid=(n_seq, GB_KV, nq),
        in_specs=[
            pl.BlockSpec((tq, qw), q_map),
            pl.BlockSpec((None, seq_len, GB_DIM), kv_map),
            pl.BlockSpec((None, seq_len, GB_DIM), kv_map),
            pl.BlockSpec((None, N_META, GB_DIM), mkv_map),
            pl.BlockSpec((None, N_META, GB_DIM), mkv_map),
            any_spec,
        ],
        out_specs=pl.BlockSpec((tq, qw), o_map),
        out_shape=jax.ShapeDtypeStruct(o_buf.shape, o_buf.dtype),
        scratch_shapes=[
            pltpu.VMEM((GB_GROUP * tq, GB_DIM), BF16),
            pltpu.VMEM((1, GB_GROUP * tq, 1), F32),
            pltpu.VMEM((1, GB_GROUP * tq, 1), F32),
            pltpu.VMEM((1, GB_GROUP * tq, GB_DIM), F32),
        ],
        input_output_aliases={5: 0},
        compiler_params=_cparams(("parallel", "parallel", "parallel")),
        name="attn_gqa_meta" if q_is_meta else "attn_gqa",
    )(bq, bk, bv, bk, bv, o_buf)
    return o_buf


def _swiglu(u, wg, wu, wd):
    a = _dot(u, wg)
    b = _dot(u, wu)
    hid = (a * jax.nn.sigmoid(a) * b).astype(BF16)
    return _dot(hid, wd)


def _out_ffn_kernel(o_ref, h_ref, wo_ref, g_ref, wg_ref, wu_ref, wd_ref, out_ref):
    h1 = h_ref[...] + _dot(o_ref[...], wo_ref[...])
    u = _rms(h1, g_ref[...]).astype(BF16)
    out_ref[...] = h1 + _swiglu(u, wg_ref[...], wu_ref[...], wd_ref[...])


def _out_ffn(o, h, wo, g, wg, wu, wd):
    rows, d = h.shape
    dff = wg.shape[1]
    tm = ROW_TILE
    row = lambda i: (i, 0)
    return pl.pallas_call(
        _out_ffn_kernel,
        grid=(rows // tm,),
        in_specs=[
            pl.BlockSpec((tm, d), row),
            pl.BlockSpec((tm, d), row),
            _const_spec((d, d)),
            _const_spec((1, d)),
            _const_spec((d, dff)),
            _const_spec((d, dff)),
            _const_spec((dff, d)),
        ],
        out_specs=pl.BlockSpec((tm, d), row),
        out_shape=jax.ShapeDtypeStruct((rows, d), F32),
        compiler_params=_cparams(("parallel",)),
        name="out_proj_ffn",
    )(o, h, wo, g, wg, wu, wd)


def _out_router_kernel(o_ref, h_ref, wo_ref, g_ref, wrh_ref, wrl_ref, h1_ref, u_ref, ri_ref):
    h1 = h_ref[...] + _dot(o_ref[...], wo_ref[...])
    h1_ref[...] = h1
    u = _rms(h1, g_ref[...])
    u_ref[...] = u
    uh = u.astype(BF16)
    ul = (u - uh.astype(F32)).astype(BF16)
    wrh = wrh_ref[...]
    logits = _dot(uh, wrh) + _dot(ul, wrh) + _dot(uh, wrl_ref[...])
    lane = lax.broadcasted_iota(jnp.int32, logits.shape, 1)
    lg = jnp.where(lane < N_EXPERTS, logits, NEG_INF)
    v1 = jnp.max(lg, axis=1, keepdims=True)
    i1 = jnp.min(jnp.where(lg == v1, lane, LANES), axis=1, keepdims=True)
    lg2 = jnp.where(lane == i1, NEG_INF, lg)
    v2 = jnp.max(lg2, axis=1, keepdims=True)
    i2 = jnp.min(jnp.where(lg2 == v2, lane, LANES), axis=1, keepdims=True)
    e = jnp.exp(v2 - v1)
    g1 = 1.0 / (1.0 + e)
    g2 = e * g1
    ri_ref[...] = jnp.where(lane == 0, i1.astype(F32),
                            jnp.where(lane == 1, i2.astype(F32),
                                      jnp.where(lane == 2, g1, jnp.where(lane == 3, g2, 0.0))))


def _out_router(o, h, wo, g, wrh, wrl, n_rows):
    d = h.shape[1]
    tm = ROW_TILE
    row = lambda i: (i, 0)
    return pl.pallas_call(
        _out_router_kernel,
        grid=(n_rows // tm,),
        in_specs=[
            pl.BlockSpec((tm, d), row),
            pl.BlockSpec((tm, d), row),
            _const_spec((d, d)),
            _const_spec((1, d)),
            _const_spec((d, LANES)),
            _const_spec((d, LANES)),
        ],
        out_specs=[
            pl.BlockSpec((tm, d), row),
            pl.BlockSpec((tm, d), row),
            pl.BlockSpec((tm, LANES), row),
        ],
        out_shape=[
            jax.ShapeDtypeStruct((n_rows, d), F32),
            jax.ShapeDtypeStruct((n_rows, d), F32),
            jax.ShapeDtypeStruct((n_rows, LANES), F32),
        ],
        compiler_params=_cparams(("parallel",)),
        name="out_proj_router",
    )(o, h, wo, g, wrh, wrl)


def _dispatch_kernel(slot_ref, u_hbm, xs_in, xs_hbm, sem, *, tm):
    del xs_in
    base = pl.program_id(0) * tm

    def copy(t, k):
        return pltpu.make_async_copy(u_hbm.at[pl.ds(base + t, 1)], xs_hbm.at[pl.ds(slot_ref[0, k, t], 1)], sem)

    def issue(t, carry):
        for k in range(TOP_K):
            copy(t, k).start()
        return carry

    def drain(t, carry):
        for k in range(TOP_K):
            copy(t, k).wait()
        return carry

    lax.fori_loop(0, tm, issue, 0)
    lax.fori_loop(0, tm, drain, 0)


def _dispatch(slots, u, xs_init, tm):
    n_rows, d = u.shape
    return pl.pallas_call(
        functools.partial(_dispatch_kernel, tm=tm),
        grid=(n_rows // tm,),
        in_specs=[
            pl.BlockSpec((1, TOP_K, tm), lambda i: (i, 0, 0), memory_space=pltpu.SMEM),
            pl.BlockSpec(memory_space=pl.ANY),
            pl.BlockSpec(memory_space=pl.ANY),
        ],
        out_specs=pl.BlockSpec(memory_space=pl.ANY),
        out_shape=jax.ShapeDtypeStruct(xs_init.shape, xs_init.dtype),
        scratch_shapes=[pltpu.SemaphoreType.DMA(())],
        input_output_aliases={2: 0},
        compiler_params=pltpu.CompilerParams(dimension_semantics=("arbitrary",), has_side_effects=True),
        name="moe_dispatch",
    )(slots, u, xs_init)


def _moe_ffn_kernel(te_ref, nu_ref, x_ref, wg_ref, wu_ref, wd_ref, y_ref):
    del te_ref
    used = pl.program_id(0) < nu_ref[0]

    @pl.when(used)
    def _():
        y_ref[...] = _swiglu(x_ref[...].astype(BF16), wg_ref[0], wu_ref[0], wd_ref[0])

    @pl.when(jnp.logical_not(used))
    def _():
        y_ref[...] = jnp.zeros(y_ref.shape, y_ref.dtype)


def _moe_ffn(tile_expert, n_used, xs, wg, wu, wd, tm):
    n_slots, d = xs.shape
    dff = wg.shape[2]
    x_map = lambda i, te, nu: (jnp.minimum(i, nu[0] - 1), 0)
    w_map = lambda i, te, nu: (te[i], 0, 0)
    grid_spec = pltpu.PrefetchScalarGridSpec(
        num_scalar_prefetch=2,
        grid=(n_slots // tm,),
        in_specs=[
            pl.BlockSpec((tm, d), x_map),
            pl.BlockSpec((1, d, dff), w_map, pipeline_mode=pl.Buffered(1)),
            pl.BlockSpec((1, d, dff), w_map, pipeline_mode=pl.Buffered(1)),
            pl.BlockSpec((1, dff, d), w_map, pipeline_mode=pl.Buffered(1)),
        ],
        out_specs=pl.BlockSpec((tm, d), lambda i, te, nu: (i, 0)),
    )
    return pl.pallas_call(
        _moe_ffn_kernel,
        grid_spec=grid_spec,
        out_shape=jax.ShapeDtypeStruct((n_slots, d), F32),
        compiler_params=_cparams(("arbitrary",)),
        name="moe_ffn",
    )(tile_expert, n_used, xs, wg, wu, wd)


def _combine_kernel(slot_ref, h1_ref, ri_ref, g_ref, ys_hbm, out_ref, buf_ref, sem, *, tm, final_norm):
    def copy(t, k):
        return pltpu.make_async_copy(ys_hbm.at[pl.ds(slot_ref[0, k, t], 1)], buf_ref.at[k, pl.ds(t, 1)], sem)

    def issue(t, carry):
        for k in range(TOP_K):
            copy(t, k).start()
        return carry

    def drain(t, carry):
        for k in range(TOP_K):
            copy(t, k).wait()
        return carry

    lax.fori_loop(0, tm, issue, 0)
    lax.fori_loop(0, tm, drain, 0)
    ri = ri_ref[...]
    h2 = h1_ref[...] + ri[:, 2:3] * buf_ref[0] + ri[:, 3:4] * buf_ref[1]
    out_ref[...] = _rms(h2, g_ref[...]) if final_norm else h2


def _combine(slots, h1, ri, g, ys, tm, final_norm):
    n_rows, d = h1.shape
    row = lambda i: (i, 0)
    return pl.pallas_call(
        functools.partial(_combine_kernel, tm=tm, final_norm=final_norm),
        grid=(n_rows // tm,),
        in_specs=[
            pl.BlockSpec((1, TOP_K, tm), lambda i: (i, 0, 0), memory_space=pltpu.SMEM),
            pl.BlockSpec((tm, d), row),
            pl.BlockSpec((tm, LANES), row),
            _const_spec((1, d)),
            pl.BlockSpec(memory_space=pl.ANY),
        ],
        out_specs=pl.BlockSpec((tm, d), row),
        out_shape=jax.ShapeDtypeStruct((n_rows, d), F32),
        scratch_shapes=[pltpu.VMEM((TOP_K, tm, d), F32), pltpu.SemaphoreType.DMA(())],
        compiler_params=_cparams(("arbitrary",)),
        name="moe_combine_norm",
    )(slots, h1, ri, g, ys)


def _final_norm_kernel(h_ref, g_ref, out_ref):
    out_ref[...] = _rms(h_ref[...], g_ref[...])


def _final_norm(h, g, n_rows):
    d = h.shape[1]
    tm = ROW_TILE
    row = lambda i: (i, 0)
    return pl.pallas_call(
        _final_norm_kernel,
        grid=(n_rows // tm,),
        in_specs=[pl.BlockSpec((tm, d), row), _const_spec((1, d))],
        out_specs=pl.BlockSpec((tm, d), row),
        out_shape=jax.ShapeDtypeStruct((n_rows, d), F32),
        compiler_params=_cparams(("parallel",)),
        name="final_norm",
    )(h, g)


def _rope_tables(groups, pad_rows):
    n_freq = GB_DIM // 4
    inv = ROPE_THETA ** (-jnp.arange(n_freq, dtype=F32) * 2.0 / (GB_DIM // 2))
    rows, cols = [], []
    for n_seq, seq_len in groups:
        t = jnp.arange(seq_len, dtype=jnp.int32)
        rows.append(jnp.tile((t // GRID_W).astype(F32), n_seq))
        cols.append(jnp.tile((t % GRID_W).astype(F32), n_seq))
    n_total_seq = sum(n for n, _ in groups)
    meta = jnp.tile(jnp.arange(N_META, dtype=F32) - N_META, n_total_seq)
    pad = jnp.zeros((pad_rows,), F32)
    r = jnp.concatenate(rows + [meta, pad])
    c = jnp.concatenate(cols + [meta, pad])
    ang = jnp.concatenate([r[:, None] * inv, c[:, None] * inv], axis=-1)
    cos, sin = jnp.cos(ang), jnp.sin(ang)
    reps = LANES // GB_DIM
    return jnp.tile(jnp.concatenate([cos, cos], axis=-1), (1, reps)), jnp.tile(jnp.concatenate([-sin, sin], axis=-1), (1, reps))


def _half_split_perm():
    return jnp.concatenate([jnp.arange(0, GB_DIM, 2), jnp.arange(1, GB_DIM, 2)])


def _route(ri, tm):
    n = ri.shape[0]
    experts = ri[:, :TOP_K].astype(jnp.int32).T.reshape(-1)
    onehot = (experts[:, None] == jnp.arange(N_EXPERTS, dtype=jnp.int32)[None, :]).astype(jnp.int32)
    incl = jnp.cumsum(onehot, axis=0)
    counts = incl[-1]
    pos = jnp.sum((incl - onehot) * onehot, axis=1)
    padded = ((counts + tm - 1) // tm) * tm
    ends = jnp.cumsum(padded)
    starts = ends - padded
    slot = jnp.sum(onehot * starts[None, :], axis=1) + pos
    slots = slot.reshape(TOP_K, n // tm, tm).transpose(1, 0, 2)
    n_slots = TOP_K * n + N_EXPERTS * tm
    tile_start = jnp.arange(n_slots // tm, dtype=jnp.int32) * tm
    tile_expert = jnp.sum((tile_start[:, None] >= ends[None, :]).astype(jnp.int32), axis=1)
    n_used = (ends[-1] // tm).astype(jnp.int32)
    last_used_expert = jnp.sum((jnp.maximum(ends[-1] - tm, 0) >= ends).astype(jnp.int32))
    tile_expert = jnp.where(tile_start < ends[-1], tile_expert, last_used_expert).astype(jnp.int32)
    return slots.astype(jnp.int32), tile_expert, n_used.reshape(1), n_slots


def kernel(x_prompt, x_sample, meta_tokens, norm1_g, w_in, lambda_q1, lambda_k1, lambda_q2, lambda_k2, subln_g, q_norm_g, k_norm_g, w_out, norm2_g, ffn_w_gate, ffn_w_up, ffn_w_down, router_w, moe_w_gate, moe_w_up, moe_w_down, final_norm_g):
    depth = w_in.shape[0]
    d = x_prompt.shape[-1]
    groups = [(x_prompt.shape[0], x_prompt.shape[1]), (x_sample.shape[0], x_sample.shape[1])]
    n_seq_total = sum(n for n, _ in groups)
    n_real = sum(n * s for n, s in groups)
    assert n_real % ROW_TILE == 0
    meta_rows = n_seq_total * N_META
    meta_pad = -(-meta_rows // ROW_TILE) * ROW_TILE
    n_rows = n_real + meta_pad

    h = jnp.concatenate([
        x_prompt.reshape(-1, d), x_sample.reshape(-1, d),
        jnp.tile(meta_tokens.astype(F32), (n_seq_total, 1)),
        jnp.zeros((meta_pad - meta_rows, d), F32)], axis=0)

    cs, sn = _rope_tables(groups, meta_pad - meta_rows)
    perm = _half_split_perm()
    c0 = 3 * DA_WIDTH
    col_perm = jnp.concatenate(
        [jnp.arange(c0)]
        + [c0 + hd * GB_DIM + perm for hd in range(GB_HEADS + GB_KV)]
        + [jnp.arange(c0 + QK_NORM_WIDTH, w_in.shape[2])])
    head_id = jnp.arange(QK_NORM_WIDTH) // GB_DIM
    block_diag = (head_id[:, None] == head_id[None, :]).astype(BF16)
    slopes = (2.0 ** (-8.0 / DA_HEADS)) ** jnp.arange(1, DA_HEADS + 1, dtype=F32)
    nslope = -slopes * math.log2(math.e)

    out = None
    for layer in range(depth):
        last = layer == depth - 1
        lambda_init = 0.8 - 0.6 * math.exp(-0.3 * layer)
        lam = (jnp.exp(jnp.sum(lambda_q1[layer].astype(F32) * lambda_k1[layer].astype(F32)))
               - jnp.exp(jnp.sum(lambda_q2[layer].astype(F32) * lambda_k2[layer].astype(F32))) + lambda_init)
        gqk = jnp.concatenate([jnp.tile(q_norm_g[layer][perm], GB_HEADS), jnp.tile(k_norm_g[layer][perm], GB_KV)])
        aq, ak, av, bq, bk, bv = _in_proj(
            h, norm1_g[layer][None], w_in[layer][:, col_perm].astype(BF16), cs, sn, gqk[None].astype(F32), block_diag)

        o = jnp.zeros((n_rows, DA_WIDTH + GB_WIDTH), BF16)
        row0, seq0 = 0, 0
        for n_seq, seq_len in groups:
            for q_is_meta in ((False,) if last else (False, True)):
                o = _attention(o, aq, ak, av, bq, bk, bv, nslope, lam.reshape(1), subln_g[layer][None].astype(F32),
                               1.0 - lambda_init, n_seq=n_seq, seq_len=seq_len, row0=row0, seq0=seq0,
                               n_real=n_real, q_is_meta=q_is_meta)
            row0 += n_seq * seq_len
            seq0 += n_seq

        wo = w_out[layer].astype(BF16)
        g2 = norm2_g[layer][None]
        if layer % 2 == 0:
            i = layer // 2
            if last:
                h = _out_ffn(o[:n_real], h[:n_real], wo, g2, ffn_w_gate[i].astype(BF16), ffn_w_up[i].astype(BF16),
                             ffn_w_down[i].astype(BF16))
                out = _final_norm(h, final_norm_g[None], n_real)
            else:
                h = _out_ffn(o, h, wo, g2, ffn_w_gate[i].astype(BF16), ffn_w_up[i].astype(BF16),
                             ffn_w_down[i].astype(BF16))
        else:
            i = layer // 2
            n_tok = n_real if last else n_rows
            wr = jnp.zeros((d, LANES), F32).at[:, :N_EXPERTS].set(router_w[i])
            wrh = wr.astype(BF16)
            wrl = (wr - wrh.astype(F32)).astype(BF16)
            h1, u, ri = _out_router(o, h, wo, g2, wrh, wrl, n_tok)
            tm = ROW_TILE
            slots, tile_expert, n_used, n_slots = _route(ri, tm)
            xs = _dispatch(slots, u, jnp.zeros((n_slots, d), F32), tm)
            ys = _moe_ffn(tile_expert, n_used, xs, moe_w_gate[i].astype(BF16), moe_w_up[i].astype(BF16),
                          moe_w_down[i].astype(BF16), tm)
            res = _combine(slots, h1, ri, final_norm_g[None], ys, tm, final_norm=last)
            if last:
                out = res
            else:
                h = res

    y_prompt = out[:groups[0][0] * groups[0][1]].reshape(x_prompt.shape)
    y_sample = out[groups[0][0] * groups[0][1]:n_real].reshape(x_sample.shape)
    return (y_prompt, y_sample)
```

```python
import functools
import math

import jax
import jax.numpy as jnp
from jax import lax
from jax.experimental import pallas as pl
from jax.experimental.pallas import tpu as pltpu

F32 = jnp.float32
BF16 = jnp.bfloat16

N_META = 16
GRID_W = 64
DA_HEADS = 4
DA_DIM = 64
DA_WIDTH = DA_HEADS * 2 * DA_DIM
GB_HEADS = 8
GB_KV = 2
GB_GROUP = GB_HEADS // GB_KV
GB_DIM = 64
GB_WIDTH = GB_HEADS * GB_DIM
GB_KV_WIDTH = GB_KV * GB_DIM
ROPE_THETA = 10000.0
N_EXPERTS = 8
TOP_K = 2
EPS = 1e-6
QK_NORM_WIDTH = GB_WIDTH + GB_KV_WIDTH
LANES = 128
ROW_TILE = 256
ATTN_TILE = ROW_TILE
ATTN_B_TQ = 128
META_TK = 512
POS_W = 64
KX_W = DA_DIM + POS_W
VMEM_LIMIT = 56 * 1024 * 1024
Q_SCALE = (DA_DIM ** -0.5) * math.log2(math.e)
NEG_INF = float("-inf")


def _cparams(sem):
    return pltpu.CompilerParams(dimension_semantics=sem, vmem_limit_bytes=VMEM_LIMIT)


def _const_spec(shape):
    nd = len(shape)
    return pl.BlockSpec(shape, lambda *_: (0,) * nd, pipeline_mode=pl.Buffered(1))


def _rms(x, g):
    return x * lax.rsqrt(jnp.mean(x * x, axis=-1, keepdims=True) + EPS) * g


def _dot(a, b):
    return jnp.dot(a, b, preferred_element_type=F32)


def _dot_nt(a, b):
    return lax.dot_general(a, b, (((1,), (1,)), ((), ())), preferred_element_type=F32)


def _in_proj_kernel(h_ref, g_ref, w_ref, wvt_ref, cs_ref, sn_ref, kpos_ref, gqk_ref, bd_ref,
                    aq_ref, akx_ref, av_ref, avt_ref, bq_ref, bk_ref, bv_ref, bvt_ref):
    u = _rms(h_ref[...], g_ref[...]).astype(BF16)
    y = _dot(u, w_ref[...])
    aq_ref[...] = (y[:, :DA_WIDTH] * Q_SCALE).astype(BF16)
    kpos = kpos_ref[...]
    for b in range(2 * DA_HEADS):
        kmap = y[:, DA_WIDTH + b * DA_DIM:DA_WIDTH + (b + 1) * DA_DIM].astype(BF16)
        akx_ref[:, b * KX_W:(b + 1) * KX_W] = jnp.concatenate([kmap, kpos], axis=1)
    av_ref[...] = y[:, 2 * DA_WIDTH:3 * DA_WIDTH].astype(BF16)
    yt = _dot_nt(wvt_ref[...], u)
    avt_ref[...] = yt[:DA_WIDTH].astype(BF16)
    bvt_ref[...] = yt[DA_WIDTH:].astype(BF16)
    c0 = 3 * DA_WIDTH
    t = y[:, c0:c0 + QK_NORM_WIDTH]
    t2 = t * t
    hi = t2.astype(BF16)
    lo = (t2 - hi.astype(F32)).astype(BF16)
    bd = bd_ref[...]
    ss = _dot(hi, bd) + _dot(lo, bd)
    n = t * lax.rsqrt(ss * (1.0 / GB_DIM) + EPS) * gqk_ref[...]
    lane = lax.broadcasted_iota(jnp.int32, (t.shape[0], LANES), 1)
    first_half = (lane % GB_DIM) < (GB_DIM // 2)
    cs = cs_ref[...]
    sn = sn_ref[...]
    for c in range(QK_NORM_WIDTH // LANES):
        xc = n[:, c * LANES:(c + 1) * LANES]
        partner = jnp.where(first_half, pltpu.roll(xc, LANES - GB_DIM // 2, 1), pltpu.roll(xc, GB_DIM // 2, 1))
        r = xc * cs + partner * sn
        if c < GB_WIDTH // LANES:
            bq_ref[:, c * LANES:(c + 1) * LANES] = (r * Q_SCALE).astype(BF16)
        else:
            for kv in range(GB_KV):
                bk_ref[kv] = r[:, kv * GB_DIM:(kv + 1) * GB_DIM].astype(BF16)
    c1 = c0 + QK_NORM_WIDTH
    for kv in range(GB_KV):
        bv_ref[kv] = y[:, c1 + kv * GB_DIM:c1 + (kv + 1) * GB_DIM].astype(BF16)


def _in_proj(h, g, w, wvt, cs, sn, kpos, gqk, bd):
    rows, d = h.shape
    d_in = w.shape[1]
    tm = ROW_TILE
    assert tm == ATTN_TILE
    row = lambda i: (i, 0)
    kv_rows = lambda i: (0, i, 0)
    vt_tile = lambda i: (i, 0, 0)
    akx_w = 2 * DA_HEADS * KX_W
    return pl.pallas_call(
        _in_proj_kernel,
        grid=(rows // tm,),
        in_specs=[
            pl.BlockSpec((tm, d), row),
            _const_spec((1, d)),
            _const_spec((d, d_in)),
            _const_spec(wvt.shape),
            pl.BlockSpec((tm, LANES), row),
            pl.BlockSpec((tm, LANES), row),
            pl.BlockSpec((tm, POS_W), row),
            _const_spec((1, QK_NORM_WIDTH)),
            _const_spec((QK_NORM_WIDTH, QK_NORM_WIDTH)),
        ],
        out_specs=[
            pl.BlockSpec((tm, DA_WIDTH), row),
            pl.BlockSpec((tm, akx_w), row),
            pl.BlockSpec((tm, DA_WIDTH), row),
            pl.BlockSpec((None, DA_WIDTH, tm), vt_tile),
            pl.BlockSpec((tm, GB_WIDTH), row),
            pl.BlockSpec((GB_KV, tm, GB_DIM), kv_rows),
            pl.BlockSpec((GB_KV, tm, GB_DIM), kv_rows),
            pl.BlockSpec((None, GB_KV_WIDTH, tm), vt_tile),
        ],
        out_shape=[
            jax.ShapeDtypeStruct((rows, DA_WIDTH), BF16),
            jax.ShapeDtypeStruct((rows, akx_w), BF16),
            jax.ShapeDtypeStruct((rows, DA_WIDTH), BF16),
            jax.ShapeDtypeStruct((rows // tm, DA_WIDTH, tm), BF16),
            jax.ShapeDtypeStruct((rows, GB_WIDTH), BF16),
            jax.ShapeDtypeStruct((GB_KV, rows, GB_DIM), BF16),
            jax.ShapeDtypeStruct((GB_KV, rows, GB_DIM), BF16),
            jax.ShapeDtypeStruct((rows // tm, GB_KV_WIDTH, tm), BF16),
        ],
        compiler_params=_cparams(("parallel",)),
        name="in_proj",
    )(h, g, w, wvt, cs, sn, kpos, gqk, bd)


def _softmax_step(s, v, m_ref, l_ref, acc_ref, j):
    m_prev = m_ref[j]
    m_new = jnp.maximum(m_prev, jnp.max(s, axis=1, keepdims=True))
    alpha = jnp.exp2(m_prev - m_new)
    p = jnp.exp2(s - m_new)
    l_ref[j] = alpha * l_ref[j] + jnp.sum(p, axis=1, keepdims=True)
    acc_ref[j] = alpha * acc_ref[j] + _dot(p.astype(BF16), v)
    m_ref[j] = m_new


def _attn_a_kernel(nslope_ref, lam_ref, q_ref, k_ref, v_ref, mk_ref, mv_ref, sg_ref, oin_ref,
                   o_ref, m_ref, l_ref, acc_ref, *, tq, tk, nk, q_is_meta, out_scale):
    del oin_ref
    nslope = nslope_ref[pl.program_id(1)]
    q = q_ref[...]
    qs = (q[:, :DA_DIM], q[:, DA_DIM:])
    row = lax.broadcasted_iota(jnp.int32, (tq, 1), 0)
    if q_is_meta:
        qpos = row.astype(F32)
    else:
        qpos = (row + (pl.program_id(2) * tq + N_META)).astype(F32)
    m_ref[...] = jnp.full(m_ref.shape, NEG_INF, F32)
    l_ref[...] = jnp.zeros(l_ref.shape, F32)
    acc_ref[...] = jnp.zeros(acc_ref.shape, F32)

    def tile(kb, vb, kpos):
        bias = jnp.abs(qpos - kpos) * nslope
        for j in range(2):
            s = _dot_nt(qs[j], kb[:, j * KX_W:j * KX_W + DA_DIM]) + bias
            _softmax_step(s, vb, m_ref, l_ref, acc_ref, j)

    def body(i, carry):
        k0 = pl.multiple_of(i * tk, tk)
        col = lax.broadcasted_iota(jnp.int32, (1, tk), 1)
        tile(k_ref[pl.ds(k0, tk), :], v_ref[pl.ds(k0, tk), :], (col + (k0 + N_META)).astype(F32))
        return carry

    lax.fori_loop(0, nk, body, 0)
    tile(mk_ref[...], mv_ref[...], lax.broadcasted_iota(jnp.int32, (1, N_META), 1).astype(F32))
    o = acc_ref[0] / l_ref[0] - lam_ref[0] * (acc_ref[1] / l_ref[1])
    o_ref[...] = (_rms(o, sg_ref[...]) * out_scale).astype(o_ref.dtype)


def _attn_b_kernel(q_ref, k_ref, v_ref, mk_ref, mv_ref, oin_ref,
                   o_ref, qs_ref, m_ref, l_ref, acc_ref, *, tq, tk, nk):
    del oin_ref
    for g in range(GB_GROUP):
        qs_ref[g * tq:(g + 1) * tq, :] = q_ref[:, g * GB_DIM:(g + 1) * GB_DIM]
    qs = qs_ref[...]
    m_ref[...] = jnp.full(m_ref.shape, NEG_INF, F32)
    l_ref[...] = jnp.zeros(l_ref.shape, F32)
    acc_ref[...] = jnp.zeros(acc_ref.shape, F32)

    def tile(kb, vb):
        _softmax_step(_dot_nt(qs, kb), vb, m_ref, l_ref, acc_ref, 0)

    def body(i, carry):
        k0 = pl.multiple_of(i * tk, tk)
        tile(k_ref[pl.ds(k0, tk), :], v_ref[pl.ds(k0, tk), :])
        return carry

    lax.fori_loop(0, nk, body, 0)
    tile(mk_ref[...], mv_ref[...])
    o = acc_ref[0] / l_ref[0]
    for g in range(GB_GROUP):
        o_ref[:, g * GB_DIM:(g + 1) * GB_DIM] = o[g * tq:(g + 1) * tq].astype(o_ref.dtype)


def _attention_meta(o_buf, aq, akx, av, bq, bk, bv, nslope, lam, subln_g, out_scale, *,
                    n_seq, seq_len, row0, seq0, n_real):
    assert row0 % seq_len == 0 and seq_len % META_TK == 0
    nk = seq_len // META_TK
    kv_blk0 = row0 // seq_len
    meta_blk0 = n_real // N_META + seq0
    any_spec = pl.BlockSpec(memory_space=pl.ANY)
    smem = pl.BlockSpec(memory_space=pltpu.SMEM)

    tq = N_META
    nq = 1
    q_map = lambda b, h, i: (meta_blk0 + b, h)
    kv_map = lambda b, h, i: (kv_blk0 + b, h)
    mkv_map = lambda b, h, i: (meta_blk0 + b, h)
    w = 2 * DA_DIM
    o_buf = pl.pallas_call(
        functools.partial(_attn_a_kernel, tq=tq, tk=META_TK, nk=nk, q_is_meta=True, out_scale=out_scale),
        grid=(n_seq, DA_HEADS, nq),
        in_specs=[
            smem, smem,
            pl.BlockSpec((tq, w), q_map),
            pl.BlockSpec((seq_len, 2 * KX_W), kv_map),
            pl.BlockSpec((seq_len, w), kv_map),
            pl.BlockSpec((N_META, 2 * KX_W), mkv_map),
            pl.BlockSpec((N_META, w), mkv_map),
            pl.BlockSpec((1, w), lambda b, h, i: (0, 0)),
            any_spec,
        ],
        out_specs=pl.BlockSpec((tq, w), q_map),
        out_shape=jax.ShapeDtypeStruct(o_buf.shape, o_buf.dtype),
        scratch_shapes=[
            pltpu.VMEM((2, tq, 1), F32),
            pltpu.VMEM((2, tq, 1), F32),
            pltpu.VMEM((2, tq, w), F32),
        ],
        input_output_aliases={8: 0},
        compiler_params=_cparams(("parallel", "parallel", "parallel")),
        name="attn_diff_meta",
    )(nslope, lam, aq, akx, av, akx, av, subln_g, o_buf)

    qw = GB_GROUP * GB_DIM
    ocol0 = DA_WIDTH // qw
    q_map = lambda b, c, i: (meta_blk0 + b, c)
    o_map = lambda b, c, i: (meta_blk0 + b, ocol0 + c)
    kv_map = lambda b, c, i: (c, kv_blk0 + b, 0)
    mkv_map = lambda b, c, i: (c, meta_blk0 + b, 0)
    o_buf = pl.pallas_call(
        functools.partial(_attn_b_kernel, tq=tq, tk=META_TK, nk=nk),
        grid=(n_seq, GB_KV, nq),
        in_specs=[
            pl.BlockSpec((tq, qw), q_map),
            pl.BlockSpec((None, seq_len, GB_DIM), kv_map),
            pl.BlockSpec((None, seq_len, GB_DIM), kv_map),
            pl.BlockSpec((None, N_META, GB_DIM), mkv_map),
            pl.BlockSpec((None, N_META, GB_DIM), mkv_map),
            any_spec,
        ],
        out_specs=pl.BlockSpec((tq, qw), o_map),
        out_shape=jax.ShapeDtypeStruct(o_buf.shape, o_buf.dtype),
        scratch_shapes=[
            pltpu.VMEM((GB_GROUP * tq, GB_DIM), BF16),
            pltpu.VMEM((1, GB_GROUP * tq, 1), F32),
            pltpu.VMEM((1, GB_GROUP * tq, 1), F32),
            pltpu.VMEM((1, GB_GROUP * tq, GB_DIM), F32),
        ],
        input_output_aliases={5: 0},
        compiler_params=_cparams(("parallel", "parallel", "parallel")),
        name="attn_gqa_meta",
    )(bq, bk, bv, bk, bv, o_buf)
    return o_buf


def _own_meta_rows(shape, seq):
    r = lax.broadcasted_iota(jnp.int32, shape, 0)
    return (r >= seq * N_META) & (r < (seq + 1) * N_META)


def _split3(x):
    hi = x.astype(BF16).astype(F32)
    r = x - hi
    mid = r.astype(BF16).astype(F32)
    return hi, mid, (r - mid).astype(BF16).astype(F32)


def _attn_a_t_kernel(nslope_ref, lam_ref, q_ref, k_ref, vt_ref, mk_ref, mvt_ref, sg_ref, oin_ref,
                     o_ref, qx_ref, s_ref, cmax_ref, p_ref, alpha_ref, m_ref, l_ref, acc_ref, *,
                     t, nk, seq0, out_scale):
    del oin_ref
    qi = pl.program_id(2)
    nslope = nslope_ref[pl.program_id(1)]
    q = q_ref[...]
    qpos = (lax.broadcasted_iota(jnp.int32, (t, 1), 0) + (qi * t + N_META)).astype(F32)
    lane = lax.broadcasted_iota(jnp.int32, (t, POS_W), 1)
    c1, c2, c3 = _split3(jnp.full((t, POS_W), -nslope, F32))
    b1, b2, b3 = _split3(jnp.broadcast_to(nslope * qpos, (t, POS_W)))
    coef = jnp.where((lane == 0) | (lane == 3), c1, jnp.where((lane == 1) | (lane == 4), c2,
           jnp.where((lane == 2) | (lane == 5), c3, jnp.where(lane == 6, b1, jnp.where(lane == 7, b2,
           jnp.where(lane == 8, b3, 0.0))))))
    for v, sign in enumerate((1.0, 0.0, -1.0)):
        ext = (coef * sign).astype(BF16)
        for j in range(2):
            qx_ref[2 * v + j] = jnp.concatenate([q[:, j * DA_DIM:(j + 1) * DA_DIM], ext], axis=1)
    m_ref[...] = jnp.full(m_ref.shape, NEG_INF, F32)
    l_ref[...] = jnp.zeros(l_ref.shape, F32)
    acc_ref[...] = jnp.zeros(acc_ref.shape, F32)

    def scores(kb, v, slot, bias=None, mask=None):
        for j in range(2):
            s = _dot_nt(kb[:, j * KX_W:(j + 1) * KX_W], qx_ref[2 * v + j])
            if bias is not None:
                s = s + bias
            if mask is not None:
                s = jnp.where(mask, s, NEG_INF)
            s_ref[slot, j] = s
            cmax_ref[slot, j] = jnp.max(s, axis=0, keepdims=True)

    def softmax(slot):
        for j in range(2):
            m_prev = m_ref[j]
            m_new = jnp.maximum(m_prev, cmax_ref[slot, j])
            alpha = jnp.exp2(m_prev - m_new)
            p = jnp.exp2(s_ref[slot, j] - m_new)
            l_ref[j] = alpha * l_ref[j] + jnp.sum(p, axis=0, keepdims=True)
            m_ref[j] = m_new
            alpha_ref[slot, j] = alpha
            p_ref[slot, j] = p.astype(BF16)

    def accumulate(vtb, slot):
        for j in range(2):
            acc_ref[j] = alpha_ref[slot, j] * acc_ref[j] + _dot(vtb, p_ref[slot, j])

    def other(n):
        return jnp.where(n >= qi, n + 1, n)

    def other_scores(n, slot):
        i = other(n)
        scores(k_ref[pl.ds(pl.multiple_of(i * t, t), t), :], jnp.where(i < qi, 0, 2), slot)

    kk = lax.broadcasted_iota(jnp.int32, (t, t), 0)
    qq = lax.broadcasted_iota(jnp.int32, (t, t), 1)
    scores(k_ref[pl.ds(pl.multiple_of(qi * t, t), t), :], 1, 0, bias=jnp.abs(kk - qq).astype(F32) * nslope)
    other_scores(0, 1)
    softmax(0)

    def pair(jj, carry):
        p = 2 * jj + 1
        accumulate(vt_ref[jnp.where(p == 1, qi, other(p - 2))], 0)
        other_scores(p, 0)
        softmax(1)
        accumulate(vt_ref[other(p - 1)], 1)
        other_scores(p + 1, 1)
        softmax(0)
        return carry

    lax.fori_loop(0, (nk - 2) // 2, pair, 0)
    accumulate(vt_ref[qi if nk == 2 else other(nk - 3)], 0)
    scores(mk_ref[...], 0, 0, mask=_own_meta_rows((t, t), seq0 + pl.program_id(0)))
    softmax(1)
    accumulate(vt_ref[other(nk - 2)], 1)
    softmax(0)
    accumulate(mvt_ref[...], 0)

    ot = acc_ref[0] / l_ref[0] - lam_ref[0] * (acc_ref[1] / l_ref[1])
    o_ref[...] = (_rms(ot.T, sg_ref[...]) * out_scale).astype(o_ref.dtype)


def _attn_b_t_kernel(q_ref, k_ref, vt_ref, mk_ref, mvt_ref, oin_ref,
                     o_ref, qs_ref, s_ref, cmax_ref, p_ref, alpha_ref, m_ref, l_ref, acc_ref, *, tq, t, nk, seq0):
    del oin_ref
    assert nk % 2 == 0
    for g in range(GB_GROUP):
        qs_ref[g * tq:(g + 1) * tq, :] = q_ref[:, g * GB_DIM:(g + 1) * GB_DIM]
    qs = qs_ref[...]
    nq = GB_GROUP * tq
    m_ref[...] = jnp.full(m_ref.shape, NEG_INF, F32)
    l_ref[...] = jnp.zeros(l_ref.shape, F32)
    acc_ref[...] = jnp.zeros(acc_ref.shape, F32)

    def scores(kb, slot, mask=None):
        s = _dot_nt(kb, qs)
        if mask is not None:
            s = jnp.where(mask, s, NEG_INF)
        s_ref[slot] = s
        cmax_ref[slot] = jnp.max(s, axis=0, keepdims=True)

    def seq_scores(i, slot):
        scores(k_ref[pl.ds(pl.multiple_of(i * t, t), t), :], slot)

    def softmax(slot):
        m_prev = m_ref[...]
        m_new = jnp.maximum(m_prev, cmax_ref[slot])
        alpha = jnp.exp2(m_prev - m_new)
        p = jnp.exp2(s_ref[slot] - m_new)
        l_ref[...] = alpha * l_ref[...] + jnp.sum(p, axis=0, keepdims=True)
        m_ref[...] = m_new
        alpha_ref[slot] = alpha
        p_ref[slot] = p.astype(BF16)

    def accumulate(vtb, slot):
        acc_ref[...] = alpha_ref[slot] * acc_ref[...] + _dot(vtb, p_ref[slot])

    seq_scores(0, 0)
    seq_scores(1, 1)
    softmax(0)

    def pair(jj, carry):
        p = 2 * jj + 1
        accumulate(vt_ref[p - 1], 0)
        seq_scores(p + 1, 0)
        softmax(1)
        accumulate(vt_ref[p], 1)
        seq_scores(p + 2, 1)
        softmax(0)
        return carry

    lax.fori_loop(0, (nk - 2) // 2, pair, 0)
    accumulate(vt_ref[nk - 2], 0)
    scores(mk_ref[...], 0, mask=_own_meta_rows((t, nq), seq0 + pl.program_id(0)))
    softmax(1)
    accumulate(vt_ref[nk - 1], 1)
    softmax(0)
    accumulate(mvt_ref[...], 0)
    ot = acc_ref[...] / l_ref[...]
    for g in range(0, GB_GROUP, 2):
        pair = jnp.concatenate([ot[:, g * tq:(g + 1) * tq], ot[:, (g + 1) * tq:(g + 2) * tq]], axis=0)
        o_ref[:, g * GB_DIM:(g + 2) * GB_DIM] = pair.T.astype(o_ref.dtype)


def _attention_main(o_buf, aq, akx, avt, bq, bk, bvt, nslope, lam, subln_g, out_scale, *,
                    n_seq, seq_len, row0, seq0, n_real):
    t = ATTN_TILE
    assert row0 % seq_len == 0 and seq_len % t == 0 and n_real % t == 0
    nk = seq_len // t
    kv_blk0 = row0 // seq_len
    meta_tile = n_real // t
    any_spec = pl.BlockSpec(memory_space=pl.ANY)
    smem = pl.BlockSpec(memory_space=pltpu.SMEM)

    q_blk0 = row0 // t
    q_map = lambda b, h, i: (q_blk0 + b * nk + i, h)
    w = 2 * DA_DIM
    o_buf = pl.pallas_call(
        functools.partial(_attn_a_t_kernel, t=t, nk=nk, seq0=seq0, out_scale=out_scale),
        grid=(n_seq, DA_HEADS, nk),
        in_specs=[
            smem, smem,
            pl.BlockSpec((t, w), q_map),
            pl.BlockSpec((seq_len, 2 * KX_W), lambda b, h, i: (kv_blk0 + b, h)),
            pl.BlockSpec((nk, w, t), lambda b, h, i: (kv_blk0 + b, h, 0)),
            pl.BlockSpec((t, 2 * KX_W), lambda b, h, i: (meta_tile, h)),
            pl.BlockSpec((None, w, t), lambda b, h, i: (meta_tile, h, 0)),
            pl.BlockSpec((1, w), lambda b, h, i: (0, 0)),
            any_spec,
        ],
        out_specs=pl.BlockSpec((t, w), q_map),
        out_shape=jax.ShapeDtypeStruct(o_buf.shape, o_buf.dtype),
        scratch_shapes=[
            pltpu.VMEM((6, t, KX_W), BF16),
            pltpu.VMEM((2, 2, t, t), F32),
            pltpu.VMEM((2, 2, 1, t), F32),
            pltpu.VMEM((2, 2, t, t), BF16),
            pltpu.VMEM((2, 2, 1, t), F32),
            pltpu.VMEM((2, 1, t), F32),
            pltpu.VMEM((2, 1, t), F32),
            pltpu.VMEM((2, w, t), F32),
        ],
        input_output_aliases={8: 0},
        compiler_params=_cparams(("parallel", "parallel", "parallel")),
        name="attn_diff",
    )(nslope, lam, aq, akx, avt, akx, avt, subln_g, o_buf)

    tq = ATTN_B_TQ
    nq = seq_len // tq
    qw = GB_GROUP * GB_DIM
    ocol0 = DA_WIDTH // qw
    q_blk0 = row0 // tq
    o_buf = pl.pallas_call(
        functools.partial(_attn_b_t_kernel, tq=tq, t=t, nk=nk, seq0=seq0),
        grid=(n_seq, GB_KV, nq),
        in_specs=[
            pl.BlockSpec((tq, qw), lambda b, c, i: (q_blk0 + b * nq + i, c)),
            pl.BlockSpec((None, seq_len, GB_DIM), lambda b, c, i: (c, kv_blk0 + b, 0)),
            pl.BlockSpec((nk, GB_DIM, t), lambda b, c, i: (kv_blk0 + b, c, 0)),
            pl.BlockSpec((None, t, GB_DIM), lambda b, c, i: (c, meta_tile, 0)),
            pl.BlockSpec((None, GB_DIM, t), lambda b, c, i: (meta_tile, c, 0)),
            any_spec,
        ],
        out_specs=pl.BlockSpec((tq, qw), lambda b, c, i: (q_blk0 + b * nq + i, ocol0 + c)),
        out_shape=jax.ShapeDtypeStruct(o_buf.shape, o_buf.dtype),
        scratch_shapes=[
            pltpu.VMEM((GB_GROUP * tq, GB_DIM), BF16),
            pltpu.VMEM((2, t, GB_GROUP * tq), F32),
            pltpu.VMEM((2, 1, GB_GROUP * tq), F32),
            pltpu.VMEM((2, t, GB_GROUP * tq), BF16),
            pltpu.VMEM((2, 1, GB_GROUP * tq), F32),
            pltpu.VMEM((1, GB_GROUP * tq), F32),
            pltpu.VMEM((1, GB_GROUP * tq), F32),
            pltpu.VMEM((GB_DIM, GB_GROUP * tq), F32),
        ],
        input_output_aliases={5: 0},
        compiler_params=_cparams(("parallel", "parallel", "parallel")),
        name="attn_gqa",
    )(bq, bk, bvt, bk, bvt, o_buf)
    return o_buf


def _swiglu(u, wg, wu, wd):
    a = _dot(u, wg)
    b = _dot(u, wu)
    hid = (a * jax.nn.sigmoid(a) * b).astype(BF16)
    return _dot(hid, wd)


def _out_ffn_kernel(o_ref, h_ref, wo_ref, g_ref, wg_ref, wu_ref, wd_ref, out_ref):
    h1 = h_ref[...] + _dot(o_ref[...], wo_ref[...])
    u = _rms(h1, g_ref[...]).astype(BF16)
    out_ref[...] = h1 + _swiglu(u, wg_ref[...], wu_ref[...], wd_ref[...])


def _out_ffn(o, h, wo, g, wg, wu, wd):
    rows, d = h.shape
    dff = wg.shape[1]
    tm = ROW_TILE
    row = lambda i: (i, 0)
    return pl.pallas_call(
        _out_ffn_kernel,
        grid=(rows // tm,),
        in_specs=[
            pl.BlockSpec((tm, d), row),
            pl.BlockSpec((tm, d), row),
            _const_spec((d, d)),
            _const_spec((1, d)),
            _const_spec((d, dff)),
            _const_spec((d, dff)),
            _const_spec((dff, d)),
        ],
        out_specs=pl.BlockSpec((tm, d), row),
        out_shape=jax.ShapeDtypeStruct((rows, d), F32),
        compiler_params=_cparams(("parallel",)),
        name="out_proj_ffn",
    )(o, h, wo, g, wg, wu, wd)


def _out_router_kernel(o_ref, h_ref, wo_ref, g_ref, wrh_ref, wrl_ref, h1_ref, u_ref, ri_ref):
    h1 = h_ref[...] + _dot(o_ref[...], wo_ref[...])
    h1_ref[...] = h1
    u = _rms(h1, g_ref[...])
    u_ref[...] = u
    uh = u.astype(BF16)
    ul = (u - uh.astype(F32)).astype(BF16)
    wrh = wrh_ref[...]
    logits = _dot(uh, wrh) + _dot(ul, wrh) + _dot(uh, wrl_ref[...])
    lane = lax.broadcasted_iota(jnp.int32, logits.shape, 1)
    lg = jnp.where(lane < N_EXPERTS, logits, NEG_INF)
    v1 = jnp.max(lg, axis=1, keepdims=True)
    i1 = jnp.min(jnp.where(lg == v1, lane, LANES), axis=1, keepdims=True)
    lg2 = jnp.where(lane == i1, NEG_INF, lg)
    v2 = jnp.max(lg2, axis=1, keepdims=True)
    i2 = jnp.min(jnp.where(lg2 == v2, lane, LANES), axis=1, keepdims=True)
    e = jnp.exp(v2 - v1)
    g1 = 1.0 / (1.0 + e)
    g2 = e * g1
    ri_ref[...] = jnp.where(lane == 0, i1.astype(F32),
                            jnp.where(lane == 1, i2.astype(F32),
                                      jnp.where(lane == 2, g1, jnp.where(lane == 3, g2, 0.0))))


def _out_router(o, h, wo, g, wrh, wrl, n_rows):
    d = h.shape[1]
    tm = ROW_TILE
    row = lambda i: (i, 0)
    return pl.pallas_call(
        _out_router_kernel,
        grid=(n_rows // tm,),
        in_specs=[
            pl.BlockSpec((tm, d), row),
            pl.BlockSpec((tm, d), row),
            _const_spec((d, d)),
            _const_spec((1, d)),
            _const_spec((d, LANES)),
            _const_spec((d, LANES)),
        ],
        out_specs=[
            pl.BlockSpec((tm, d), row),
            pl.BlockSpec((tm, d), row),
            pl.BlockSpec((tm, LANES), row),
        ],
        out_shape=[
            jax.ShapeDtypeStruct((n_rows, d), F32),
            jax.ShapeDtypeStruct((n_rows, d), F32),
            jax.ShapeDtypeStruct((n_rows, LANES), F32),
        ],
        compiler_params=_cparams(("parallel",)),
        name="out_proj_router",
    )(o, h, wo, g, wrh, wrl)


def _dispatch_kernel(slot_ref, u_ref, xs_in, xs_hbm, sem, *, tm):
    del xs_in

    def copy(t, k):
        return pltpu.make_async_copy(u_ref.at[pl.ds(t, 1)], xs_hbm.at[pl.ds(slot_ref[0, k, t], 1)], sem)

    def issue(t, carry):
        for k in range(TOP_K):
            copy(t, k).start()
        return carry

    def drain(t, carry):
        for k in range(TOP_K):
            copy(t, k).wait()
        return carry

    lax.fori_loop(0, tm, issue, 0)
    lax.fori_loop(0, tm, drain, 0)


def _dispatch(slots, u, xs_init, tm):
    n_rows, d = u.shape
    return pl.pallas_call(
        functools.partial(_dispatch_kernel, tm=tm),
        grid=(n_rows // tm,),
        in_specs=[
            pl.BlockSpec((1, TOP_K, tm), lambda i: (i, 0, 0), memory_space=pltpu.SMEM),
            pl.BlockSpec((tm, d), lambda i: (i, 0)),
            pl.BlockSpec(memory_space=pl.ANY),
        ],
        out_specs=pl.BlockSpec(memory_space=pl.ANY),
        out_shape=jax.ShapeDtypeStruct(xs_init.shape, xs_init.dtype),
        scratch_shapes=[pltpu.SemaphoreType.DMA(())],
        input_output_aliases={2: 0},
        compiler_params=pltpu.CompilerParams(dimension_semantics=("arbitrary",), has_side_effects=True,
                                             vmem_limit_bytes=VMEM_LIMIT),
        name="moe_dispatch",
    )(slots, u, xs_init)


def _moe_ffn_kernel(te_ref, nu_ref, x_ref, wg_ref, wu_ref, wd_ref, y_ref):
    del te_ref
    used = pl.program_id(0) < nu_ref[0]

    @pl.when(used)
    def _():
        y_ref[...] = _swiglu(x_ref[...].astype(BF16), wg_ref[0], wu_ref[0], wd_ref[0])

    @pl.when(jnp.logical_not(used))
    def _():
        y_ref[...] = jnp.zeros(y_ref.shape, y_ref.dtype)


def _moe_ffn(tile_expert, n_used, xs, wg, wu, wd, tm):
    n_slots, d = xs.shape
    dff = wg.shape[2]
    x_map = lambda i, te, nu: (jnp.minimum(i, nu[0] - 1), 0)
    w_map = lambda i, te, nu: (te[i], 0, 0)
    grid_spec = pltpu.PrefetchScalarGridSpec(
        num_scalar_prefetch=2,
        grid=(n_slots // tm,),
        in_specs=[
            pl.BlockSpec((tm, d), x_map),
            pl.BlockSpec((1, d, dff), w_map, pipeline_mode=pl.Buffered(1)),
            pl.BlockSpec((1, d, dff), w_map, pipeline_mode=pl.Buffered(1)),
            pl.BlockSpec((1, dff, d), w_map, pipeline_mode=pl.Buffered(1)),
        ],
        out_specs=pl.BlockSpec((tm, d), lambda i, te, nu: (i, 0)),
    )
    return pl.pallas_call(
        _moe_ffn_kernel,
        grid_spec=grid_spec,
        out_shape=jax.ShapeDtypeStruct((n_slots, d), F32),
        compiler_params=_cparams(("arbitrary",)),
        name="moe_ffn",
    )(tile_expert, n_used, xs, wg, wu, wd)


def _combine_kernel(slot_ref, h1_ref, ri_ref, g_ref, ys_hbm, out_ref, buf_ref, sem, *, tm, final_norm):
    def copy(t, k):
        return pltpu.make_async_copy(ys_hbm.at[pl.ds(slot_ref[0, k, t], 1)], buf_ref.at[k, pl.ds(t, 1)], sem)

    def issue(t, carry):
        for k in range(TOP_K):
            copy(t, k).start()
        return carry

    def drain(t, carry):
        for k in range(TOP_K):
            copy(t, k).wait()
        return carry

    lax.fori_loop(0, tm, issue, 0)
    lax.fori_loop(0, tm, drain, 0)
    ri = ri_ref[...]
    h2 = h1_ref[...] + ri[:, 2:3] * buf_ref[0] + ri[:, 3:4] * buf_ref[1]
    out_ref[...] = _rms(h2, g_ref[...]) if final_norm else h2


def _combine(slots, h1, ri, g, ys, tm, final_norm):
    n_rows, d = h1.shape
    row = lambda i: (i, 0)
    return pl.pallas_call(
        functools.partial(_combine_kernel, tm=tm, final_norm=final_norm),
        grid=(n_rows // tm,),
        in_specs=[
            pl.BlockSpec((1, TOP_K, tm), lambda i: (i, 0, 0), memory_space=pltpu.SMEM),
            pl.BlockSpec((tm, d), row),
            pl.BlockSpec((tm, LANES), row),
            _const_spec((1, d)),
            pl.BlockSpec(memory_space=pl.ANY),
        ],
        out_specs=pl.BlockSpec((tm, d), row),
        out_shape=jax.ShapeDtypeStruct((n_rows, d), F32),
        scratch_shapes=[pltpu.VMEM((TOP_K, tm, d), F32), pltpu.SemaphoreType.DMA(())],
        compiler_params=_cparams(("arbitrary",)),
        name="moe_combine_norm",
    )(slots, h1, ri, g, ys)


def _final_norm_kernel(h_ref, g_ref, out_ref):
    out_ref[...] = _rms(h_ref[...], g_ref[...])


def _final_norm(h, g, n_rows):
    d = h.shape[1]
    tm = ROW_TILE
    row = lambda i: (i, 0)
    return pl.pallas_call(
        _final_norm_kernel,
        grid=(n_rows // tm,),
        in_specs=[pl.BlockSpec((tm, d), row), _const_spec((1, d))],
        out_specs=pl.BlockSpec((tm, d), row),
        out_shape=jax.ShapeDtypeStruct((n_rows, d), F32),
        compiler_params=_cparams(("parallel",)),
        name="final_norm",
    )(h, g)


def _rope_tables(groups, pad_rows):
    n_freq = GB_DIM // 4
    inv = ROPE_THETA ** (-jnp.arange(n_freq, dtype=F32) * 2.0 / (GB_DIM // 2))
    rows, cols = [], []
    for n_seq, seq_len in groups:
        t = jnp.arange(seq_len, dtype=jnp.int32)
        rows.append(jnp.tile((t // GRID_W).astype(F32), n_seq))
        cols.append(jnp.tile((t % GRID_W).astype(F32), n_seq))
    n_total_seq = sum(n for n, _ in groups)
    meta = jnp.tile(jnp.arange(N_META, dtype=F32) - N_META, n_total_seq)
    pad = jnp.zeros((pad_rows,), F32)
    r = jnp.concatenate(rows + [meta, pad])
    c = jnp.concatenate(cols + [meta, pad])
    ang = jnp.concatenate([r[:, None] * inv, c[:, None] * inv], axis=-1)
    cos, sin = jnp.cos(ang), jnp.sin(ang)
    reps = LANES // GB_DIM
    return jnp.tile(jnp.concatenate([cos, cos], axis=-1), (1, reps)), jnp.tile(jnp.concatenate([-sin, sin], axis=-1), (1, reps))


def _key_position_columns(groups, pad_rows):
    n_total_seq = sum(n for n, _ in groups)
    pos = jnp.concatenate(
        [jnp.tile(jnp.arange(s, dtype=jnp.int32) + N_META, n) for n, s in groups]
        + [jnp.tile(jnp.arange(N_META, dtype=jnp.int32), n_total_seq), jnp.zeros((pad_rows,), jnp.int32)])
    hi = ((pos // LANES) * LANES).astype(F32)[:, None]
    lo = (pos % LANES).astype(F32)[:, None]
    col = jnp.arange(POS_W)[None, :]
    table = jnp.where(col < 3, hi, jnp.where(col < 6, lo, jnp.where(col < 9, 1.0, 0.0)))
    return table.astype(BF16)


def _half_split_perm():
    return jnp.concatenate([jnp.arange(0, GB_DIM, 2), jnp.arange(1, GB_DIM, 2)])


def _route(ri, tm):
    n = ri.shape[0]
    experts = ri[:, :TOP_K].astype(jnp.int32).T.reshape(-1)
    onehot = (experts[:, None] == jnp.arange(N_EXPERTS, dtype=jnp.int32)[None, :]).astype(jnp.int32)
    incl = jnp.cumsum(onehot, axis=0)
    counts = incl[-1]
    pos = jnp.sum((incl - onehot) * onehot, axis=1)
    padded = ((counts + tm - 1) // tm) * tm
    ends = jnp.cumsum(padded)
    starts = ends - padded
    slot = jnp.sum(onehot * starts[None, :], axis=1) + pos
    slots = slot.reshape(TOP_K, n // tm, tm).transpose(1, 0, 2)
    n_slots = TOP_K * n + N_EXPERTS * tm
    tile_start = jnp.arange(n_slots // tm, dtype=jnp.int32) * tm
    tile_expert = jnp.sum((tile_start[:, None] >= ends[None, :]).astype(jnp.int32), axis=1)
    n_used = (ends[-1] // tm).astype(jnp.int32)
    last_used_expert = jnp.sum((jnp.maximum(ends[-1] - tm, 0) >= ends).astype(jnp.int32))
    tile_expert = jnp.where(tile_start < ends[-1], tile_expert, last_used_expert).astype(jnp.int32)
    return slots.astype(jnp.int32), tile_expert, n_used.reshape(1), n_slots


def kernel(x_prompt, x_sample, meta_tokens, norm1_g, w_in, lambda_q1, lambda_k1, lambda_q2, lambda_k2, subln_g, q_norm_g, k_norm_g, w_out, norm2_g, ffn_w_gate, ffn_w_up, ffn_w_down, router_w, moe_w_gate, moe_w_up, moe_w_down, final_norm_g):
    depth = w_in.shape[0]
    d = x_prompt.shape[-1]
    groups = [(x_prompt.shape[0], x_prompt.shape[1]), (x_sample.shape[0], x_sample.shape[1])]
    n_seq_total = sum(n for n, _ in groups)
    n_real = sum(n * s for n, s in groups)
    assert n_real % ROW_TILE == 0
    meta_rows = n_seq_total * N_META
    meta_pad = -(-meta_rows // ROW_TILE) * ROW_TILE
    n_rows = n_real + meta_pad

    h = jnp.concatenate([
        x_prompt.reshape(-1, d), x_sample.reshape(-1, d),
        jnp.tile(meta_tokens.astype(F32), (n_seq_total, 1)),
        jnp.zeros((meta_pad - meta_rows, d), F32)], axis=0)

    cs, sn = _rope_tables(groups, meta_pad - meta_rows)
    kpos = _key_position_columns(groups, meta_pad - meta_rows)
    perm = _half_split_perm()
    c0 = 3 * DA_WIDTH
    col_perm = jnp.concatenate(
        [jnp.arange(c0)]
        + [c0 + hd * GB_DIM + perm for hd in range(GB_HEADS + GB_KV)]
        + [jnp.arange(c0 + QK_NORM_WIDTH, w_in.shape[2])])
    head_id = jnp.arange(QK_NORM_WIDTH) // GB_DIM
    block_diag = (head_id[:, None] == head_id[None, :]).astype(BF16)
    slopes = (2.0 ** (-8.0 / DA_HEADS)) ** jnp.arange(1, DA_HEADS + 1, dtype=F32)
    nslope = -slopes * math.log2(math.e)

    out = None
    for layer in range(depth):
        last = layer == depth - 1
        lambda_init = 0.8 - 0.6 * math.exp(-0.3 * layer)
        lam = (jnp.exp(jnp.sum(lambda_q1[layer].astype(F32) * lambda_k1[layer].astype(F32)))
               - jnp.exp(jnp.sum(lambda_q2[layer].astype(F32) * lambda_k2[layer].astype(F32))) + lambda_init)
        gqk = jnp.concatenate([jnp.tile(q_norm_g[layer][perm], GB_HEADS), jnp.tile(k_norm_g[layer][perm], GB_KV)])
        w_l = w_in[layer]
        wvt = jnp.concatenate([w_l[:, 2 * DA_WIDTH:3 * DA_WIDTH], w_l[:, c0 + QK_NORM_WIDTH:]], axis=1).T
        aq, akx, av, avt, bq, bk, bv, bvt = _in_proj(
            h, norm1_g[layer][None], w_l[:, col_perm].astype(BF16), wvt.astype(BF16), cs, sn, kpos,
            gqk[None].astype(F32), block_diag)

        o = jnp.zeros((n_rows, DA_WIDTH + GB_WIDTH), BF16)
        row0, seq0 = 0, 0
        for n_seq, seq_len in groups:
            where = dict(n_seq=n_seq, seq_len=seq_len, row0=row0, seq0=seq0, n_real=n_real)
            sg = subln_g[layer][None].astype(F32)
            o = _attention_main(o, aq, akx, avt, bq, bk, bvt, nslope, lam.reshape(1), sg, 1.0 - lambda_init, **where)
            if not last:
                o = _attention_meta(o, aq, akx, av, bq, bk, bv, nslope, lam.reshape(1), sg, 1.0 - lambda_init, **where)
            row0 += n_seq * seq_len
            seq0 += n_seq

        wo = w_out[layer].astype(BF16)
        g2 = norm2_g[layer][None]
        if layer % 2 == 0:
            i = layer // 2
            if last:
                h = _out_ffn(o[:n_real], h[:n_real], wo, g2, ffn_w_gate[i].astype(BF16), ffn_w_up[i].astype(BF16),
                             ffn_w_down[i].astype(BF16))
                out = _final_norm(h, final_norm_g[None], n_real)
            else:
                h = _out_ffn(o, h, wo, g2, ffn_w_gate[i].astype(BF16), ffn_w_up[i].astype(BF16),
                             ffn_w_down[i].astype(BF16))
        else:
            i = layer // 2
            n_tok = n_real if last else n_rows
            wr = jnp.zeros((d, LANES), F32).at[:, :N_EXPERTS].set(router_w[i])
            wrh = wr.astype(BF16)
            wrl = (wr - wrh.astype(F32)).astype(BF16)
            h1, u, ri = _out_router(o, h, wo, g2, wrh, wrl, n_tok)
            tm = ROW_TILE
            slots, tile_expert, n_used, n_slots = _route(ri, tm)
            xs = _dispatch(slots, u, jnp.zeros((n_slots, d), F32), tm)
            ys = _moe_ffn(tile_expert, n_used, xs, moe_w_gate[i].astype(BF16), moe_w_up[i].astype(BF16),
                          moe_w_down[i].astype(BF16), tm)
            res = _combine(slots, h1, ri, final_norm_g[None], ys, tm, final_norm=last)
            if last:
                out = res
            else:
                h = res

    y_prompt = out[:groups[0][0] * groups[0][1]].reshape(x_prompt.shape)
    y_sample = out[groups[0][0] * groups[0][1]:n_real].reshape(x_sample.shape)
    return (y_prompt, y_sample)
```

```python
import functools
import math

import jax
import jax.numpy as jnp
from jax import lax
from jax.experimental import pallas as pl
from jax.experimental.pallas import tpu as pltpu

F32 = jnp.float32
BF16 = jnp.bfloat16

N_META = 16
GRID_W = 64
DA_HEADS = 4
DA_DIM = 64
DA_WIDTH = DA_HEADS * 2 * DA_DIM
GB_HEADS = 8
GB_KV = 2
GB_GROUP = GB_HEADS // GB_KV
GB_DIM = 64
GB_WIDTH = GB_HEADS * GB_DIM
GB_KV_WIDTH = GB_KV * GB_DIM
ROPE_THETA = 10000.0
N_EXPERTS = 8
TOP_K = 2
EPS = 1e-6
QK_NORM_WIDTH = GB_WIDTH + GB_KV_WIDTH
LANES = 128
ROW_TILE = 256
ATTN_TILE = ROW_TILE
ATTN_B_TQ = 256
META_TK = 512
POS_W = 64
KX_W = DA_DIM + POS_W
ONES_ROWS = 16
VMEM_LIMIT = 56 * 1024 * 1024
Q_SCALE = (DA_DIM ** -0.5) * math.log2(math.e)
NEG_INF = float("-inf")


def _cparams(sem):
    return pltpu.CompilerParams(dimension_semantics=sem, vmem_limit_bytes=VMEM_LIMIT)


def _const_spec(shape):
    nd = len(shape)
    return pl.BlockSpec(shape, lambda *_: (0,) * nd, pipeline_mode=pl.Buffered(1))


def _rms(x, g):
    return x * lax.rsqrt(jnp.mean(x * x, axis=-1, keepdims=True) + EPS) * g


def _dot(a, b):
    return jnp.dot(a, b, preferred_element_type=F32)


def _dot_nt(a, b):
    return lax.dot_general(a, b, (((1,), (1,)), ((), ())), preferred_element_type=F32)


def _in_proj_kernel(h_ref, g_ref, w_ref, wvt_ref, cs_ref, sn_ref, kpos_ref, gqk_ref, bd_ref,
                    aq_ref, akx_ref, av_ref, avt_ref, bq_ref, bk_ref, bv_ref, bvt_ref):
    u = _rms(h_ref[...], g_ref[...]).astype(BF16)
    y = _dot(u, w_ref[...])
    aq_ref[...] = (y[:, :DA_WIDTH] * Q_SCALE).astype(BF16)
    kpos = kpos_ref[...]
    for b in range(2 * DA_HEADS):
        kmap = y[:, DA_WIDTH + b * DA_DIM:DA_WIDTH + (b + 1) * DA_DIM].astype(BF16)
        akx_ref[:, b * KX_W:(b + 1) * KX_W] = jnp.concatenate([kmap, kpos], axis=1)
    av_ref[...] = y[:, 2 * DA_WIDTH:3 * DA_WIDTH].astype(BF16)
    yt = _dot_nt(wvt_ref[...], u)
    avt_ref[...] = yt[:DA_WIDTH].astype(BF16)
    bvt_ref[...] = yt[DA_WIDTH:].astype(BF16)
    c0 = 3 * DA_WIDTH
    t = y[:, c0:c0 + QK_NORM_WIDTH]
    t2 = t * t
    hi = t2.astype(BF16)
    lo = (t2 - hi.astype(F32)).astype(BF16)
    bd = bd_ref[...]
    ss = _dot(hi, bd) + _dot(lo, bd)
    n = t * lax.rsqrt(ss * (1.0 / GB_DIM) + EPS) * gqk_ref[...]
    lane = lax.broadcasted_iota(jnp.int32, (t.shape[0], LANES), 1)
    first_half = (lane % GB_DIM) < (GB_DIM // 2)
    cs = cs_ref[...]
    sn = sn_ref[...]
    for c in range(QK_NORM_WIDTH // LANES):
        xc = n[:, c * LANES:(c + 1) * LANES]
        partner = jnp.where(first_half, pltpu.roll(xc, LANES - GB_DIM // 2, 1), pltpu.roll(xc, GB_DIM // 2, 1))
        r = xc * cs + partner * sn
        if c < GB_WIDTH // LANES:
            bq_ref[:, c * LANES:(c + 1) * LANES] = (r * Q_SCALE).astype(BF16)
        else:
            for kv in range(GB_KV):
                bk_ref[kv] = r[:, kv * GB_DIM:(kv + 1) * GB_DIM].astype(BF16)
    c1 = c0 + QK_NORM_WIDTH
    for kv in range(GB_KV):
        bv_ref[kv] = y[:, c1 + kv * GB_DIM:c1 + (kv + 1) * GB_DIM].astype(BF16)


def _in_proj(h, g, w, wvt, cs, sn, kpos, gqk, bd):
    rows, d = h.shape
    d_in = w.shape[1]
    tm = ROW_TILE
    assert tm == ATTN_TILE
    row = lambda i: (i, 0)
    kv_rows = lambda i: (0, i, 0)
    vt_tile = lambda i: (i, 0, 0)
    akx_w = 2 * DA_HEADS * KX_W
    return pl.pallas_call(
        _in_proj_kernel,
        grid=(rows // tm,),
        in_specs=[
            pl.BlockSpec((tm, d), row),
            _const_spec((1, d)),
            _const_spec((d, d_in)),
            _const_spec(wvt.shape),
            pl.BlockSpec((tm, LANES), row),
            pl.BlockSpec((tm, LANES), row),
            pl.BlockSpec((tm, POS_W), row),
            _const_spec((1, QK_NORM_WIDTH)),
            _const_spec((QK_NORM_WIDTH, QK_NORM_WIDTH)),
        ],
        out_specs=[
            pl.BlockSpec((tm, DA_WIDTH), row),
            pl.BlockSpec((tm, akx_w), row),
            pl.BlockSpec((tm, DA_WIDTH), row),
            pl.BlockSpec((None, DA_WIDTH, tm), vt_tile),
            pl.BlockSpec((tm, GB_WIDTH), row),
            pl.BlockSpec((GB_KV, tm, GB_DIM), kv_rows),
            pl.BlockSpec((GB_KV, tm, GB_DIM), kv_rows),
            pl.BlockSpec((None, GB_KV_WIDTH, tm), vt_tile),
        ],
        out_shape=[
            jax.ShapeDtypeStruct((rows, DA_WIDTH), BF16),
            jax.ShapeDtypeStruct((rows, akx_w), BF16),
            jax.ShapeDtypeStruct((rows, DA_WIDTH), BF16),
            jax.ShapeDtypeStruct((rows // tm, DA_WIDTH, tm), BF16),
            jax.ShapeDtypeStruct((rows, GB_WIDTH), BF16),
            jax.ShapeDtypeStruct((GB_KV, rows, GB_DIM), BF16),
            jax.ShapeDtypeStruct((GB_KV, rows, GB_DIM), BF16),
            jax.ShapeDtypeStruct((rows // tm, GB_KV_WIDTH, tm), BF16),
        ],
        compiler_params=_cparams(("parallel",)),
        name="in_proj",
    )(h, g, w, wvt, cs, sn, kpos, gqk, bd)


def _softmax_step(s, v, m_ref, l_ref, acc_ref, j):
    m_prev = m_ref[j]
    m_new = jnp.maximum(m_prev, jnp.max(s, axis=1, keepdims=True))
    alpha = jnp.exp2(m_prev - m_new)
    p = jnp.exp2(s - m_new)
    l_ref[j] = alpha * l_ref[j] + jnp.sum(p, axis=1, keepdims=True)
    acc_ref[j] = alpha * acc_ref[j] + _dot(p.astype(BF16), v)
    m_ref[j] = m_new


def _attn_a_kernel(nslope_ref, lam_ref, q_ref, k_ref, v_ref, mk_ref, mv_ref, sg_ref, oin_ref,
                   o_ref, m_ref, l_ref, acc_ref, *, tq, tk, nk, q_is_meta, out_scale):
    del oin_ref
    nslope = nslope_ref[pl.program_id(1)]
    q = q_ref[...]
    qs = (q[:, :DA_DIM], q[:, DA_DIM:])
    row = lax.broadcasted_iota(jnp.int32, (tq, 1), 0)
    if q_is_meta:
        qpos = row.astype(F32)
    else:
        qpos = (row + (pl.program_id(2) * tq + N_META)).astype(F32)
    m_ref[...] = jnp.full(m_ref.shape, NEG_INF, F32)
    l_ref[...] = jnp.zeros(l_ref.shape, F32)
    acc_ref[...] = jnp.zeros(acc_ref.shape, F32)

    def tile(kb, vb, kpos):
        bias = jnp.abs(qpos - kpos) * nslope
        for j in range(2):
            s = _dot_nt(qs[j], kb[:, j * KX_W:j * KX_W + DA_DIM]) + bias
            _softmax_step(s, vb, m_ref, l_ref, acc_ref, j)

    def body(i, carry):
        k0 = pl.multiple_of(i * tk, tk)
        col = lax.broadcasted_iota(jnp.int32, (1, tk), 1)
        tile(k_ref[pl.ds(k0, tk), :], v_ref[pl.ds(k0, tk), :], (col + (k0 + N_META)).astype(F32))
        return carry

    lax.fori_loop(0, nk, body, 0)
    tile(mk_ref[...], mv_ref[...], lax.broadcasted_iota(jnp.int32, (1, N_META), 1).astype(F32))
    o = acc_ref[0] / l_ref[0] - lam_ref[0] * (acc_ref[1] / l_ref[1])
    o_ref[...] = (_rms(o, sg_ref[...]) * out_scale).astype(o_ref.dtype)


def _attn_b_kernel(q_ref, k_ref, v_ref, mk_ref, mv_ref, oin_ref,
                   o_ref, qs_ref, m_ref, l_ref, acc_ref, *, tq, tk, nk):
    del oin_ref
    for g in range(GB_GROUP):
        qs_ref[g * tq:(g + 1) * tq, :] = q_ref[:, g * GB_DIM:(g + 1) * GB_DIM]
    qs = qs_ref[...]
    m_ref[...] = jnp.full(m_ref.shape, NEG_INF, F32)
    l_ref[...] = jnp.zeros(l_ref.shape, F32)
    acc_ref[...] = jnp.zeros(acc_ref.shape, F32)

    def tile(kb, vb):
        _softmax_step(_dot_nt(qs, kb), vb, m_ref, l_ref, acc_ref, 0)

    def body(i, carry):
        k0 = pl.multiple_of(i * tk, tk)
        tile(k_ref[pl.ds(k0, tk), :], v_ref[pl.ds(k0, tk), :])
        return carry

    lax.fori_loop(0, nk, body, 0)
    tile(mk_ref[...], mv_ref[...])
    o = acc_ref[0] / l_ref[0]
    for g in range(GB_GROUP):
        o_ref[:, g * GB_DIM:(g + 1) * GB_DIM] = o[g * tq:(g + 1) * tq].astype(o_ref.dtype)


def _attention_meta(o_buf, aq, akx, av, bq, bk, bv, nslope, lam, subln_g, out_scale, *,
                    n_seq, seq_len, row0, seq0, n_real):
    assert row0 % seq_len == 0 and seq_len % META_TK == 0
    nk = seq_len // META_TK
    kv_blk0 = row0 // seq_len
    meta_blk0 = n_real // N_META + seq0
    any_spec = pl.BlockSpec(memory_space=pl.ANY)
    smem = pl.BlockSpec(memory_space=pltpu.SMEM)

    tq = N_META
    nq = 1
    q_map = lambda b, h, i: (meta_blk0 + b, h)
    kv_map = lambda b, h, i: (kv_blk0 + b, h)
    mkv_map = lambda b, h, i: (meta_blk0 + b, h)
    w = 2 * DA_DIM
    o_buf = pl.pallas_call(
        functools.partial(_attn_a_kernel, tq=tq, tk=META_TK, nk=nk, q_is_meta=True, out_scale=out_scale),
        grid=(n_seq, DA_HEADS, nq),
        in_specs=[
            smem, smem,
            pl.BlockSpec((tq, w), q_map),
            pl.BlockSpec((seq_len, 2 * KX_W), kv_map),
            pl.BlockSpec((seq_len, w), kv_map),
            pl.BlockSpec((N_META, 2 * KX_W), mkv_map),
            pl.BlockSpec((N_META, w), mkv_map),
            pl.BlockSpec((1, w), lambda b, h, i: (0, 0)),
            any_spec,
        ],
        out_specs=pl.BlockSpec((tq, w), q_map),
        out_shape=jax.ShapeDtypeStruct(o_buf.shape, o_buf.dtype),
        scratch_shapes=[
            pltpu.VMEM((2, tq, 1), F32),
            pltpu.VMEM((2, tq, 1), F32),
            pltpu.VMEM((2, tq, w), F32),
        ],
        input_output_aliases={8: 0},
        compiler_params=_cparams(("parallel", "parallel", "parallel")),
        name="attn_diff_meta",
    )(nslope, lam, aq, akx, av, akx, av, subln_g, o_buf)

    qw = GB_GROUP * GB_DIM
    ocol0 = DA_WIDTH // qw
    q_map = lambda b, c, i: (meta_blk0 + b, c)
    o_map = lambda b, c, i: (meta_blk0 + b, ocol0 + c)
    kv_map = lambda b, c, i: (c, kv_blk0 + b, 0)
    mkv_map = lambda b, c, i: (c, meta_blk0 + b, 0)
    o_buf = pl.pallas_call(
        functools.partial(_attn_b_kernel, tq=tq, tk=META_TK, nk=nk),
        grid=(n_seq, GB_KV, nq),
        in_specs=[
            pl.BlockSpec((tq, qw), q_map),
            pl.BlockSpec((None, seq_len, GB_DIM), kv_map),
            pl.BlockSpec((None, seq_len, GB_DIM), kv_map),
            pl.BlockSpec((None, N_META, GB_DIM), mkv_map),
            pl.BlockSpec((None, N_META, GB_DIM), mkv_map),
            any_spec,
        ],
        out_specs=pl.BlockSpec((tq, qw), o_map),
        out_shape=jax.ShapeDtypeStruct(o_buf.shape, o_buf.dtype),
        scratch_shapes=[
            pltpu.VMEM((GB_GROUP * tq, GB_DIM), BF16),
            pltpu.VMEM((1, GB_GROUP * tq, 1), F32),
            pltpu.VMEM((1, GB_GROUP * tq, 1), F32),
            pltpu.VMEM((1, GB_GROUP * tq, GB_DIM), F32),
        ],
        input_output_aliases={5: 0},
        compiler_params=_cparams(("parallel", "parallel", "parallel")),
        name="attn_gqa_meta",
    )(bq, bk, bv, bk, bv, o_buf)
    return o_buf


def _own_meta_rows(shape, seq):
    r = lax.broadcasted_iota(jnp.int32, shape, 0)
    return (r >= seq * N_META) & (r < (seq + 1) * N_META)


def _split3(x):
    hi = x.astype(BF16).astype(F32)
    r = x - hi
    mid = r.astype(BF16).astype(F32)
    return hi, mid, (r - mid).astype(BF16).astype(F32)


def _attn_a_t_kernel(nslope_ref, lam_ref, q_ref, k_ref, vt_ref, mk_ref, mvt_ref, sg_ref, oin_ref,
                     o_ref, qx_ref, s_ref, cmax_ref, p_ref, alpha_ref, m_ref, acc_ref, *,
                     t, nk, seq0, out_scale):
    del oin_ref
    qi = pl.program_id(2)
    nslope = nslope_ref[pl.program_id(1)]
    q = q_ref[...]
    qpos = (lax.broadcasted_iota(jnp.int32, (t, 1), 0) + (qi * t + N_META)).astype(F32)
    lane = lax.broadcasted_iota(jnp.int32, (t, POS_W), 1)
    c1, c2, c3 = _split3(jnp.full((t, POS_W), -nslope, F32))
    b1, b2, b3 = _split3(jnp.broadcast_to(nslope * qpos, (t, POS_W)))
    coef = jnp.where((lane == 0) | (lane == 3), c1, jnp.where((lane == 1) | (lane == 4), c2,
           jnp.where((lane == 2) | (lane == 5), c3, jnp.where(lane == 6, b1, jnp.where(lane == 7, b2,
           jnp.where(lane == 8, b3, 0.0))))))
    for v, sign in enumerate((1.0, 0.0, -1.0)):
        ext = (coef * sign).astype(BF16)
        for j in range(2):
            qx_ref[2 * v + j] = jnp.concatenate([q[:, j * DA_DIM:(j + 1) * DA_DIM], ext], axis=1)
    m_ref[...] = jnp.full(m_ref.shape, NEG_INF, F32)
    acc_ref[...] = jnp.zeros(acc_ref.shape, F32)

    def scores(kb, v, slot, bias=None, mask=None):
        for j in range(2):
            s = _dot_nt(kb[:, j * KX_W:(j + 1) * KX_W], qx_ref[2 * v + j])
            if bias is not None:
                s = s + bias
            if mask is not None:
                s = jnp.where(mask, s, NEG_INF)
            s_ref[slot, j] = s
            cmax_ref[slot, j] = jnp.max(s, axis=0, keepdims=True)

    def softmax(slot):
        for j in range(2):
            m_prev = m_ref[j]
            m_new = jnp.maximum(m_prev, cmax_ref[slot, j])
            alpha = jnp.exp2(m_prev - m_new)
            p = jnp.exp2(s_ref[slot, j] - m_new)
            m_ref[j] = m_new
            alpha_ref[slot, j] = alpha
            p_ref[slot, j] = p.astype(BF16)

    def accumulate(vtb, slot):
        vtx = jnp.concatenate([vtb, jnp.ones((ONES_ROWS, vtb.shape[1]), BF16)], axis=0)
        for j in range(2):
            acc_ref[j] = alpha_ref[slot, j] * acc_ref[j] + _dot(vtx, p_ref[slot, j])

    def other(n):
        return jnp.where(n >= qi, n + 1, n)

    def other_scores(n, slot):
        i = other(n)
        scores(k_ref[pl.ds(pl.multiple_of(i * t, t), t), :], jnp.where(i < qi, 0, 2), slot)

    kk = lax.broadcasted_iota(jnp.int32, (t, t), 0)
    qq = lax.broadcasted_iota(jnp.int32, (t, t), 1)
    scores(k_ref[pl.ds(pl.multiple_of(qi * t, t), t), :], 1, 0, bias=jnp.abs(kk - qq).astype(F32) * nslope)
    other_scores(0, 1)
    softmax(0)

    def pair(jj, carry):
        p = 2 * jj + 1
        accumulate(vt_ref[jnp.where(p == 1, qi, other(p - 2))], 0)
        other_scores(p, 0)
        softmax(1)
        accumulate(vt_ref[other(p - 1)], 1)
        other_scores(p + 1, 1)
        softmax(0)
        return carry

    lax.fori_loop(0, (nk - 2) // 2, pair, 0, unroll=3)
    accumulate(vt_ref[qi if nk == 2 else other(nk - 3)], 0)
    scores(mk_ref[...], 0, 0, mask=_own_meta_rows((t, t), seq0 + pl.program_id(0)))
    softmax(1)
    accumulate(vt_ref[other(nk - 2)], 1)
    softmax(0)
    accumulate(mvt_ref[...], 0)

    dv = 2 * DA_DIM
    a0, a1 = acc_ref[0], acc_ref[1]
    ot = a0[:dv] / a0[dv:dv + 1] - lam_ref[0] * (a1[:dv] / a1[dv:dv + 1])
    o_ref[...] = (_rms(ot.T, sg_ref[...]) * out_scale).astype(o_ref.dtype)


def _attn_b_t_kernel(q_ref, k_ref, vt_ref, mk_ref, mvt_ref, oin_ref,
                     o_ref, qs_ref, s0_ref, s1_ref, c0_ref, c1_ref, p0_ref, p1_ref, a0_ref, a1_ref,
                     m_ref, acc_ref, *, tq, t, nk, seq0):
    del oin_ref
    assert nk % 2 == 0
    s_ref, cmax_ref, p_ref, alpha_ref = (s0_ref, s1_ref), (c0_ref, c1_ref), (p0_ref, p1_ref), (a0_ref, a1_ref)
    for g in range(GB_GROUP):
        qs_ref[g * tq:(g + 1) * tq, :] = q_ref[:, g * GB_DIM:(g + 1) * GB_DIM]
    qs = qs_ref[...]
    nq = GB_GROUP * tq
    m_ref[...] = jnp.full(m_ref.shape, NEG_INF, F32)
    acc_ref[...] = jnp.zeros(acc_ref.shape, F32)

    def scores(kb, slot, mask=None):
        s = _dot_nt(kb, qs)
        if mask is not None:
            s = jnp.where(mask, s, NEG_INF)
        s_ref[slot][...] = s
        cmax_ref[slot][...] = jnp.max(s, axis=0, keepdims=True)

    def seq_scores(i, slot):
        scores(k_ref[pl.ds(pl.multiple_of(i * t, t), t), :], slot)

    def softmax(slot):
        m_prev = m_ref[...]
        m_new = jnp.maximum(m_prev, cmax_ref[slot][...])
        alpha = jnp.exp2(m_prev - m_new)
        p = jnp.exp2(s_ref[slot][...] - m_new)
        m_ref[...] = m_new
        alpha_ref[slot][...] = alpha
        p_ref[slot][...] = p.astype(BF16)

    def accumulate(vtb, slot):
        vtx = jnp.concatenate([vtb, jnp.ones((ONES_ROWS, vtb.shape[1]), BF16)], axis=0)
        acc_ref[...] = alpha_ref[slot][...] * acc_ref[...] + _dot(vtx, p_ref[slot][...])

    seq_scores(0, 0)
    seq_scores(1, 1)
    softmax(0)

    def pair(jj, carry):
        p = 2 * jj + 1
        accumulate(vt_ref[p - 1], 0)
        seq_scores(p + 1, 0)
        softmax(1)
        accumulate(vt_ref[p], 1)
        seq_scores(p + 2, 1)
        softmax(0)
        return carry

    lax.fori_loop(0, (nk - 2) // 2, pair, 0, unroll=3)
    accumulate(vt_ref[nk - 2], 0)
    scores(mk_ref[...], 0, mask=_own_meta_rows((t, nq), seq0 + pl.program_id(0)))
    softmax(1)
    accumulate(vt_ref[nk - 1], 1)
    softmax(0)
    accumulate(mvt_ref[...], 0)
    acc = acc_ref[...]
    ot = acc[:GB_DIM] / acc[GB_DIM:GB_DIM + 1]
    for g in range(0, GB_GROUP, 2):
        pair = jnp.concatenate([ot[:, g * tq:(g + 1) * tq], ot[:, (g + 1) * tq:(g + 2) * tq]], axis=0)
        o_ref[:, g * GB_DIM:(g + 2) * GB_DIM] = pair.T.astype(o_ref.dtype)


def _attention_main(o_buf, aq, akx, avt, bq, bk, bvt, nslope, lam, subln_g, out_scale, *,
                    n_seq, seq_len, row0, seq0, n_real):
    t = ATTN_TILE
    assert row0 % seq_len == 0 and seq_len % t == 0 and n_real % t == 0
    nk = seq_len // t
    kv_blk0 = row0 // seq_len
    meta_tile = n_real // t
    any_spec = pl.BlockSpec(memory_space=pl.ANY)
    smem = pl.BlockSpec(memory_space=pltpu.SMEM)

    q_blk0 = row0 // t
    q_map = lambda b, h, i: (q_blk0 + b * nk + i, h)
    w = 2 * DA_DIM
    o_buf = pl.pallas_call(
        functools.partial(_attn_a_t_kernel, t=t, nk=nk, seq0=seq0, out_scale=out_scale),
        grid=(n_seq, DA_HEADS, nk),
        in_specs=[
            smem, smem,
            pl.BlockSpec((t, w), q_map),
            pl.BlockSpec((seq_len, 2 * KX_W), lambda b, h, i: (kv_blk0 + b, h)),
            pl.BlockSpec((nk, w, t), lambda b, h, i: (kv_blk0 + b, h, 0)),
            pl.BlockSpec((t, 2 * KX_W), lambda b, h, i: (meta_tile, h)),
            pl.BlockSpec((None, w, t), lambda b, h, i: (meta_tile, h, 0)),
            pl.BlockSpec((1, w), lambda b, h, i: (0, 0)),
            any_spec,
        ],
        out_specs=pl.BlockSpec((t, w), q_map),
        out_shape=jax.ShapeDtypeStruct(o_buf.shape, o_buf.dtype),
        scratch_shapes=[
            pltpu.VMEM((6, t, KX_W), BF16),
            pltpu.VMEM((2, 2, t, t), F32),
            pltpu.VMEM((2, 2, 1, t), F32),
            pltpu.VMEM((2, 2, t, t), BF16),
            pltpu.VMEM((2, 2, 1, t), F32),
            pltpu.VMEM((2, 1, t), F32),
            pltpu.VMEM((2, w + ONES_ROWS, t), F32),
        ],
        input_output_aliases={8: 0},
        compiler_params=_cparams(("parallel", "parallel", "parallel")),
        name="attn_diff",
    )(nslope, lam, aq, akx, avt, akx, avt, subln_g, o_buf)

    tq = ATTN_B_TQ
    nq = seq_len // tq
    qw = GB_GROUP * GB_DIM
    ocol0 = DA_WIDTH // qw
    q_blk0 = row0 // tq
    o_buf = pl.pallas_call(
        functools.partial(_attn_b_t_kernel, tq=tq, t=t, nk=nk, seq0=seq0),
        grid=(n_seq, GB_KV, nq),
        in_specs=[
            pl.BlockSpec((tq, qw), lambda b, c, i: (q_blk0 + b * nq + i, c)),
            pl.BlockSpec((None, seq_len, GB_DIM), lambda b, c, i: (c, kv_blk0 + b, 0)),
            pl.BlockSpec((nk, GB_DIM, t), lambda b, c, i: (kv_blk0 + b, c, 0)),
            pl.BlockSpec((None, t, GB_DIM), lambda b, c, i: (c, meta_tile, 0)),
            pl.BlockSpec((None, GB_DIM, t), lambda b, c, i: (meta_tile, c, 0)),
            any_spec,
        ],
        out_specs=pl.BlockSpec((tq, qw), lambda b, c, i: (q_blk0 + b * nq + i, ocol0 + c)),
        out_shape=jax.ShapeDtypeStruct(o_buf.shape, o_buf.dtype),
        scratch_shapes=[
            pltpu.VMEM((GB_GROUP * tq, GB_DIM), BF16),
            pltpu.VMEM((t, GB_GROUP * tq), F32), pltpu.VMEM((t, GB_GROUP * tq), F32),
            pltpu.VMEM((1, GB_GROUP * tq), F32), pltpu.VMEM((1, GB_GROUP * tq), F32),
            pltpu.VMEM((t, GB_GROUP * tq), BF16), pltpu.VMEM((t, GB_GROUP * tq), BF16),
            pltpu.VMEM((1, GB_GROUP * tq), F32), pltpu.VMEM((1, GB_GROUP * tq), F32),
            pltpu.VMEM((1, GB_GROUP * tq), F32),
            pltpu.VMEM((GB_DIM + ONES_ROWS, GB_GROUP * tq), F32),
        ],
        input_output_aliases={5: 0},
        compiler_params=_cparams(("parallel", "parallel", "parallel")),
        name="attn_gqa",
    )(bq, bk, bvt, bk, bvt, o_buf)
    return o_buf


def _swiglu(u, wg, wu, wd):
    a = _dot(u, wg)
    b = _dot(u, wu)
    hid = (a * jax.nn.sigmoid(a) * b).astype(BF16)
    return _dot(hid, wd)


def _out_ffn_kernel(o_ref, h_ref, wo_ref, g_ref, wg_ref, wu_ref, wd_ref, out_ref):
    h1 = h_ref[...] + _dot(o_ref[...], wo_ref[...])
    u = _rms(h1, g_ref[...]).astype(BF16)
    out_ref[...] = h1 + _swiglu(u, wg_ref[...], wu_ref[...], wd_ref[...])


def _out_ffn(o, h, wo, g, wg, wu, wd):
    rows, d = h.shape
    dff = wg.shape[1]
    tm = ROW_TILE
    row = lambda i: (i, 0)
    return pl.pallas_call(
        _out_ffn_kernel,
        grid=(rows // tm,),
        in_specs=[
            pl.BlockSpec((tm, d), row),
            pl.BlockSpec((tm, d), row),
            _const_spec((d, d)),
            _const_spec((1, d)),
            _const_spec((d, dff)),
            _const_spec((d, dff)),
            _const_spec((dff, d)),
        ],
        out_specs=pl.BlockSpec((tm, d), row),
        out_shape=jax.ShapeDtypeStruct((rows, d), F32),
        compiler_params=_cparams(("parallel",)),
        name="out_proj_ffn",
    )(o, h, wo, g, wg, wu, wd)


def _out_router_kernel(o_ref, h_ref, wo_ref, g_ref, wrh_ref, wrl_ref, h1_ref, u_ref, ri_ref):
    h1 = h_ref[...] + _dot(o_ref[...], wo_ref[...])
    h1_ref[...] = h1
    u = _rms(h1, g_ref[...])
    u_ref[...] = u
    uh = u.astype(BF16)
    ul = (u - uh.astype(F32)).astype(BF16)
    wrh = wrh_ref[...]
    logits = _dot(uh, wrh) + _dot(ul, wrh) + _dot(uh, wrl_ref[...])
    lane = lax.broadcasted_iota(jnp.int32, logits.shape, 1)
    lg = jnp.where(lane < N_EXPERTS, logits, NEG_INF)
    v1 = jnp.max(lg, axis=1, keepdims=True)
    i1 = jnp.min(jnp.where(lg == v1, lane, LANES), axis=1, keepdims=True)
    lg2 = jnp.where(lane == i1, NEG_INF, lg)
    v2 = jnp.max(lg2, axis=1, keepdims=True)
    i2 = jnp.min(jnp.where(lg2 == v2, lane, LANES), axis=1, keepdims=True)
    e = jnp.exp(v2 - v1)
    g1 = 1.0 / (1.0 + e)
    g2 = e * g1
    ri_ref[...] = jnp.where(lane == 0, i1.astype(F32),
                            jnp.where(lane == 1, i2.astype(F32),
                                      jnp.where(lane == 2, g1, jnp.where(lane == 3, g2, 0.0))))


def _out_router(o, h, wo, g, wrh, wrl, n_rows):
    d = h.shape[1]
    tm = ROW_TILE
    row = lambda i: (i, 0)
    return pl.pallas_call(
        _out_router_kernel,
        grid=(n_rows // tm,),
        in_specs=[
            pl.BlockSpec((tm, d), row),
            pl.BlockSpec((tm, d), row),
            _const_spec((d, d)),
            _const_spec((1, d)),
            _const_spec((d, LANES)),
            _const_spec((d, LANES)),
        ],
        out_specs=[
            pl.BlockSpec((tm, d), row),
            pl.BlockSpec((tm, d), row),
            pl.BlockSpec((tm, LANES), row),
        ],
        out_shape=[
            jax.ShapeDtypeStruct((n_rows, d), F32),
            jax.ShapeDtypeStruct((n_rows, d), F32),
            jax.ShapeDtypeStruct((n_rows, LANES), F32),
        ],
        compiler_params=_cparams(("parallel",)),
        name="out_proj_router",
    )(o, h, wo, g, wrh, wrl)


def _dispatch_kernel(slot_ref, u_ref, xs_in, xs_hbm, sem, *, tm):
    del xs_in

    def copy(t, k):
        return pltpu.make_async_copy(u_ref.at[pl.ds(t, 1)], xs_hbm.at[pl.ds(slot_ref[0, k, t], 1)], sem)

    def issue(t, carry):
        for k in range(TOP_K):
            copy(t, k).start()
        return carry

    def drain(t, carry):
        for k in range(TOP_K):
            copy(t, k).wait()
        return carry

    lax.fori_loop(0, tm, issue, 0)
    lax.fori_loop(0, tm, drain, 0)


def _dispatch(slots, u, xs_init, tm):
    n_rows, d = u.shape
    return pl.pallas_call(
        functools.partial(_dispatch_kernel, tm=tm),
        grid=(n_rows // tm,),
        in_specs=[
            pl.BlockSpec((1, TOP_K, tm), lambda i: (i, 0, 0), memory_space=pltpu.SMEM),
            pl.BlockSpec((tm, d), lambda i: (i, 0)),
            pl.BlockSpec(memory_space=pl.ANY),
        ],
        out_specs=pl.BlockSpec(memory_space=pl.ANY),
        out_shape=jax.ShapeDtypeStruct(xs_init.shape, xs_init.dtype),
        scratch_shapes=[pltpu.SemaphoreType.DMA(())],
        input_output_aliases={2: 0},
        compiler_params=pltpu.CompilerParams(dimension_semantics=("arbitrary",), has_side_effects=True,
                                             vmem_limit_bytes=VMEM_LIMIT),
        name="moe_dispatch",
    )(slots, u, xs_init)


def _moe_ffn_kernel(te_ref, nu_ref, x_ref, wg_ref, wu_ref, wd_ref, y_ref):
    del te_ref
    used = pl.program_id(0) < nu_ref[0]

    @pl.when(used)
    def _():
        y_ref[...] = _swiglu(x_ref[...].astype(BF16), wg_ref[0], wu_ref[0], wd_ref[0])

    @pl.when(jnp.logical_not(used))
    def _():
        y_ref[...] = jnp.zeros(y_ref.shape, y_ref.dtype)


def _moe_ffn(tile_expert, n_used, xs, wg, wu, wd, tm):
    n_slots, d = xs.shape
    dff = wg.shape[2]
    x_map = lambda i, te, nu: (jnp.minimum(i, nu[0] - 1), 0)
    w_map = lambda i, te, nu: (te[i], 0, 0)
    grid_spec = pltpu.PrefetchScalarGridSpec(
        num_scalar_prefetch=2,
        grid=(n_slots // tm,),
        in_specs=[
            pl.BlockSpec((tm, d), x_map),
            pl.BlockSpec((1, d, dff), w_map, pipeline_mode=pl.Buffered(1)),
            pl.BlockSpec((1, d, dff), w_map, pipeline_mode=pl.Buffered(1)),
            pl.BlockSpec((1, dff, d), w_map, pipeline_mode=pl.Buffered(1)),
        ],
        out_specs=pl.BlockSpec((tm, d), lambda i, te, nu: (i, 0)),
    )
    return pl.pallas_call(
        _moe_ffn_kernel,
        grid_spec=grid_spec,
        out_shape=jax.ShapeDtypeStruct((n_slots, d), F32),
        compiler_params=_cparams(("arbitrary",)),
        name="moe_ffn",
    )(tile_expert, n_used, xs, wg, wu, wd)


def _combine_kernel(slot_ref, h1_ref, ri_ref, g_ref, ys_hbm, out_ref, buf_ref, sem, *, tm, final_norm):
    def copy(t, k):
        return pltpu.make_async_copy(ys_hbm.at[pl.ds(slot_ref[0, k, t], 1)], buf_ref.at[k, pl.ds(t, 1)], sem)

    def issue(t, carry):
        for k in range(TOP_K):
            copy(t, k).start()
        return carry

    def drain(t, carry):
        for k in range(TOP_K):
            copy(t, k).wait()
        return carry

    lax.fori_loop(0, tm, issue, 0)
    lax.fori_loop(0, tm, drain, 0)
    ri = ri_ref[...]
    h2 = h1_ref[...] + ri[:, 2:3] * buf_ref[0] + ri[:, 3:4] * buf_ref[1]
    out_ref[...] = _rms(h2, g_ref[...]) if final_norm else h2


def _combine(slots, h1, ri, g, ys, tm, final_norm):
    n_rows, d = h1.shape
    row = lambda i: (i, 0)
    return pl.pallas_call(
        functools.partial(_combine_kernel, tm=tm, final_norm=final_norm),
        grid=(n_rows // tm,),
        in_specs=[
            pl.BlockSpec((1, TOP_K, tm), lambda i: (i, 0, 0), memory_space=pltpu.SMEM),
            pl.BlockSpec((tm, d), row),
            pl.BlockSpec((tm, LANES), row),
            _const_spec((1, d)),
            pl.BlockSpec(memory_space=pl.ANY),
        ],
        out_specs=pl.BlockSpec((tm, d), row),
        out_shape=jax.ShapeDtypeStruct((n_rows, d), F32),
        scratch_shapes=[pltpu.VMEM((TOP_K, tm, d), F32), pltpu.SemaphoreType.DMA(())],
        compiler_params=_cparams(("arbitrary",)),
        name="moe_combine_norm",
    )(slots, h1, ri, g, ys)


def _final_norm_kernel(h_ref, g_ref, out_ref):
    out_ref[...] = _rms(h_ref[...], g_ref[...])


def _final_norm(h, g, n_rows):
    d = h.shape[1]
    tm = ROW_TILE
    row = lambda i: (i, 0)
    return pl.pallas_call(
        _final_norm_kernel,
        grid=(n_rows // tm,),
        in_specs=[pl.BlockSpec((tm, d), row), _const_spec((1, d))],
        out_specs=pl.BlockSpec((tm, d), row),
        out_shape=jax.ShapeDtypeStruct((n_rows, d), F32),
        compiler_params=_cparams(("parallel",)),
        name="final_norm",
    )(h, g)


def _rope_tables(groups, pad_rows):
    n_freq = GB_DIM // 4
    inv = ROPE_THETA ** (-jnp.arange(n_freq, dtype=F32) * 2.0 / (GB_DIM // 2))
    rows, cols = [], []
    for n_seq, seq_len in groups:
        t = jnp.arange(seq_len, dtype=jnp.int32)
        rows.append(jnp.tile((t // GRID_W).astype(F32), n_seq))
        cols.append(jnp.tile((t % GRID_W).astype(F32), n_seq))
    n_total_seq = sum(n for n, _ in groups)
    meta = jnp.tile(jnp.arange(N_META, dtype=F32) - N_META, n_total_seq)
    pad = jnp.zeros((pad_rows,), F32)
    r = jnp.concatenate(rows + [meta, pad])
    c = jnp.concatenate(cols + [meta, pad])
    ang = jnp.concatenate([r[:, None] * inv, c[:, None] * inv], axis=-1)
    cos, sin = jnp.cos(ang), jnp.sin(ang)
    reps = LANES // GB_DIM
    return jnp.tile(jnp.concatenate([cos, cos], axis=-1), (1, reps)), jnp.tile(jnp.concatenate([-sin, sin], axis=-1), (1, reps))


def _key_position_columns(groups, pad_rows):
    n_total_seq = sum(n for n, _ in groups)
    pos = jnp.concatenate(
        [jnp.tile(jnp.arange(s, dtype=jnp.int32) + N_META, n) for n, s in groups]
        + [jnp.tile(jnp.arange(N_META, dtype=jnp.int32), n_total_seq), jnp.zeros((pad_rows,), jnp.int32)])
    hi = ((pos // LANES) * LANES).astype(F32)[:, None]
    lo = (pos % LANES).astype(F32)[:, None]
    col = jnp.arange(POS_W)[None, :]
    table = jnp.where(col < 3, hi, jnp.where(col < 6, lo, jnp.where(col < 9, 1.0, 0.0)))
    return table.astype(BF16)


def _half_split_perm():
    return jnp.concatenate([jnp.arange(0, GB_DIM, 2), jnp.arange(1, GB_DIM, 2)])


def _route(ri, tm):
    n = ri.shape[0]
    experts = ri[:, :TOP_K].astype(jnp.int32).T.reshape(-1)
    onehot = (experts[:, None] == jnp.arange(N_EXPERTS, dtype=jnp.int32)[None, :]).astype(jnp.int32)
    incl = jnp.cumsum(onehot, axis=0)
    counts = incl[-1]
    pos = jnp.sum((incl - onehot) * onehot, axis=1)
    padded = ((counts + tm - 1) // tm) * tm
    ends = jnp.cumsum(padded)
    starts = ends - padded
    slot = jnp.sum(onehot * starts[None, :], axis=1) + pos
    slots = slot.reshape(TOP_K, n // tm, tm).transpose(1, 0, 2)
    n_slots = TOP_K * n + N_EXPERTS * tm
    tile_start = jnp.arange(n_slots // tm, dtype=jnp.int32) * tm
    tile_expert = jnp.sum((tile_start[:, None] >= ends[None, :]).astype(jnp.int32), axis=1)
    n_used = (ends[-1] // tm).astype(jnp.int32)
    last_used_expert = jnp.sum((jnp.maximum(ends[-1] - tm, 0) >= ends).astype(jnp.int32))
    tile_expert = jnp.where(tile_start < ends[-1], tile_expert, last_used_expert).astype(jnp.int32)
    return slots.astype(jnp.int32), tile_expert, n_used.reshape(1), n_slots


def kernel(x_prompt, x_sample, meta_tokens, norm1_g, w_in, lambda_q1, lambda_k1, lambda_q2, lambda_k2, subln_g, q_norm_g, k_norm_g, w_out, norm2_g, ffn_w_gate, ffn_w_up, ffn_w_down, router_w, moe_w_gate, moe_w_up, moe_w_down, final_norm_g):
    depth = w_in.shape[0]
    d = x_prompt.shape[-1]
    groups = [(x_prompt.shape[0], x_prompt.shape[1]), (x_sample.shape[0], x_sample.shape[1])]
    n_seq_total = sum(n for n, _ in groups)
    n_real = sum(n * s for n, s in groups)
    assert n_real % ROW_TILE == 0
    meta_rows = n_seq_total * N_META
    meta_pad = -(-meta_rows // ROW_TILE) * ROW_TILE
    n_rows = n_real + meta_pad

    h = jnp.concatenate([
        x_prompt.reshape(-1, d), x_sample.reshape(-1, d),
        jnp.tile(meta_tokens.astype(F32), (n_seq_total, 1)),
        jnp.zeros((meta_pad - meta_rows, d), F32)], axis=0)

    cs, sn = _rope_tables(groups, meta_pad - meta_rows)
    kpos = _key_position_columns(groups, meta_pad - meta_rows)
    perm = _half_split_perm()
    c0 = 3 * DA_WIDTH
    col_perm = jnp.concatenate(
        [jnp.arange(c0)]
        + [c0 + hd * GB_DIM + perm for hd in range(GB_HEADS + GB_KV)]
        + [jnp.arange(c0 + QK_NORM_WIDTH, w_in.shape[2])])
    head_id = jnp.arange(QK_NORM_WIDTH) // GB_DIM
    block_diag = (head_id[:, None] == head_id[None, :]).astype(BF16)
    slopes = (2.0 ** (-8.0 / DA_HEADS)) ** jnp.arange(1, DA_HEADS + 1, dtype=F32)
    nslope = -slopes * math.log2(math.e)

    out = None
    for layer in range(depth):
        last = layer == depth - 1
        lambda_init = 0.8 - 0.6 * math.exp(-0.3 * layer)
        lam = (jnp.exp(jnp.sum(lambda_q1[layer].astype(F32) * lambda_k1[layer].astype(F32)))
               - jnp.exp(jnp.sum(lambda_q2[layer].astype(F32) * lambda_k2[layer].astype(F32))) + lambda_init)
        gqk = jnp.concatenate([jnp.tile(q_norm_g[layer][perm], GB_HEADS), jnp.tile(k_norm_g[layer][perm], GB_KV)])
        w_l = w_in[layer]
        wvt = jnp.concatenate([w_l[:, 2 * DA_WIDTH:3 * DA_WIDTH], w_l[:, c0 + QK_NORM_WIDTH:]], axis=1).T
        aq, akx, av, avt, bq, bk, bv, bvt = _in_proj(
            h, norm1_g[layer][None], w_l[:, col_perm].astype(BF16), wvt.astype(BF16), cs, sn, kpos,
            gqk[None].astype(F32), block_diag)

        o = jnp.zeros((n_rows, DA_WIDTH + GB_WIDTH), BF16)
        row0, seq0 = 0, 0
        for n_seq, seq_len in groups:
            where = dict(n_seq=n_seq, seq_len=seq_len, row0=row0, seq0=seq0, n_real=n_real)
            sg = subln_g[layer][None].astype(F32)
            o = _attention_main(o, aq, akx, avt, bq, bk, bvt, nslope, lam.reshape(1), sg, 1.0 - lambda_init, **where)
            if not last:
                o = _attention_meta(o, aq, akx, av, bq, bk, bv, nslope, lam.reshape(1), sg, 1.0 - lambda_init, **where)
            row0 += n_seq * seq_len
            seq0 += n_seq

        wo = w_out[layer].astype(BF16)
        g2 = norm2_g[layer][None]
        if layer % 2 == 0:
            i = layer // 2
            if last:
                h = _out_ffn(o[:n_real], h[:n_real], wo, g2, ffn_w_gate[i].astype(BF16), ffn_w_up[i].astype(BF16),
                             ffn_w_down[i].astype(BF16))
                out = _final_norm(h, final_norm_g[None], n_real)
            else:
                h = _out_ffn(o, h, wo, g2, ffn_w_gate[i].astype(BF16), ffn_w_up[i].astype(BF16),
                             ffn_w_down[i].astype(BF16))
        else:
            i = layer // 2
            n_tok = n_real if last else n_rows
            wr = jnp.zeros((d, LANES), F32).at[:, :N_EXPERTS].set(router_w[i])
            wrh = wr.astype(BF16)
            wrl = (wr - wrh.astype(F32)).astype(BF16)
            h1, u, ri = _out_router(o, h, wo, g2, wrh, wrl, n_tok)
            tm = ROW_TILE
            slots, tile_expert, n_used, n_slots = _route(ri, tm)
            xs = _dispatch(slots, u, jnp.zeros((n_slots, d), F32), tm)
            ys = _moe_ffn(tile_expert, n_used, xs, moe_w_gate[i].astype(BF16), moe_w_up[i].astype(BF16),
                          moe_w_down[i].astype(BF16), tm)
            res = _combine(slots, h1, ri, final_norm_g[None], ys, tm, final_norm=last)
            if last:
                out = res
            else:
                h = res

    y_prompt = out[:groups[0][0] * groups[0][1]].reshape(x_prompt.shape)
    y_sample = out[groups[0][0] * groups[0][1]:n_real].reshape(x_sample.shape)
    return (y_prompt, y_sample)
```

```python
import functools
import math

import jax
import jax.numpy as jnp
from jax import lax
from jax.experimental import pallas as pl
from jax.experimental.pallas import tpu as pltpu

F32 = jnp.float32
BF16 = jnp.bfloat16

N_META = 16
GRID_W = 64
DA_HEADS = 4
DA_DIM = 64
DA_WIDTH = DA_HEADS * 2 * DA_DIM
GB_HEADS = 8
GB_KV = 2
GB_GROUP = GB_HEADS // GB_KV
GB_DIM = 64
GB_WIDTH = GB_HEADS * GB_DIM
GB_KV_WIDTH = GB_KV * GB_DIM
ROPE_THETA = 10000.0
N_EXPERTS = 8
TOP_K = 2
EPS = 1e-6
QK_NORM_WIDTH = GB_WIDTH + GB_KV_WIDTH
LANES = 128
ROW_TILE = 256
ATTN_TILE = ROW_TILE
ATTN_B_TQ = 512
META_TK = 512
POS_W = 64
KX_W = DA_DIM + POS_W
ONES_ROWS = 16
VMEM_LIMIT = 56 * 1024 * 1024
Q_SCALE = (DA_DIM ** -0.5) * math.log2(math.e)
NEG_INF = float("-inf")


def _cparams(sem):
    return pltpu.CompilerParams(dimension_semantics=sem, vmem_limit_bytes=VMEM_LIMIT)


def _const_spec(shape):
    nd = len(shape)
    return pl.BlockSpec(shape, lambda *_: (0,) * nd, pipeline_mode=pl.Buffered(1))


def _rms(x, g):
    return x * lax.rsqrt(jnp.mean(x * x, axis=-1, keepdims=True) + EPS) * g


def _dot(a, b):
    return jnp.dot(a, b, preferred_element_type=F32)


def _dot_nt(a, b):
    return lax.dot_general(a, b, (((1,), (1,)), ((), ())), preferred_element_type=F32)


def _in_proj_kernel(h_ref, g_ref, w_ref, wvt_ref, cs_ref, sn_ref, kpos_ref, gqk_ref, bd_ref,
                    aq_ref, akx_ref, av_ref, avt_ref, bq_ref, bk_ref, bv_ref, bvt_ref):
    u = _rms(h_ref[...], g_ref[...]).astype(BF16)
    y = _dot(u, w_ref[...])
    aq_ref[...] = (y[:, :DA_WIDTH] * Q_SCALE).astype(BF16)
    kpos = kpos_ref[...]
    for b in range(2 * DA_HEADS):
        kmap = y[:, DA_WIDTH + b * DA_DIM:DA_WIDTH + (b + 1) * DA_DIM].astype(BF16)
        akx_ref[:, b * KX_W:(b + 1) * KX_W] = jnp.concatenate([kmap, kpos], axis=1)
    av_ref[...] = y[:, 2 * DA_WIDTH:3 * DA_WIDTH].astype(BF16)
    yt = _dot_nt(wvt_ref[...], u)
    avt_ref[...] = yt[:DA_WIDTH].astype(BF16)
    bvt_ref[...] = yt[DA_WIDTH:].astype(BF16)
    c0 = 3 * DA_WIDTH
    t = y[:, c0:c0 + QK_NORM_WIDTH]
    t2 = t * t
    hi = t2.astype(BF16)
    lo = (t2 - hi.astype(F32)).astype(BF16)
    bd = bd_ref[...]
    ss = _dot(hi, bd) + _dot(lo, bd)
    n = t * lax.rsqrt(ss * (1.0 / GB_DIM) + EPS) * gqk_ref[...]
    lane = lax.broadcasted_iota(jnp.int32, (t.shape[0], LANES), 1)
    first_half = (lane % GB_DIM) < (GB_DIM // 2)
    cs = cs_ref[...]
    sn = sn_ref[...]
    for c in range(QK_NORM_WIDTH // LANES):
        xc = n[:, c * LANES:(c + 1) * LANES]
        partner = jnp.where(first_half, pltpu.roll(xc, LANES - GB_DIM // 2, 1), pltpu.roll(xc, GB_DIM // 2, 1))
        r = xc * cs + partner * sn
        if c < GB_WIDTH // LANES:
            bq_ref[:, c * LANES:(c + 1) * LANES] = (r * Q_SCALE).astype(BF16)
        else:
            for kv in range(GB_KV):
                bk_ref[kv] = r[:, kv * GB_DIM:(kv + 1) * GB_DIM].astype(BF16)
    c1 = c0 + QK_NORM_WIDTH
    for kv in range(GB_KV):
        bv_ref[kv] = y[:, c1 + kv * GB_DIM:c1 + (kv + 1) * GB_DIM].astype(BF16)


def _in_proj(h, g, w, wvt, cs, sn, kpos, gqk, bd):
    rows, d = h.shape
    d_in = w.shape[1]
    tm = ROW_TILE
    assert tm == ATTN_TILE
    row = lambda i: (i, 0)
    kv_rows = lambda i: (0, i, 0)
    vt_tile = lambda i: (i, 0, 0)
    akx_w = 2 * DA_HEADS * KX_W
    return pl.pallas_call(
        _in_proj_kernel,
        grid=(rows // tm,),
        in_specs=[
            pl.BlockSpec((tm, d), row),
            _const_spec((1, d)),
            _const_spec((d, d_in)),
            _const_spec(wvt.shape),
            pl.BlockSpec((tm, LANES), row),
            pl.BlockSpec((tm, LANES), row),
            pl.BlockSpec((tm, POS_W), row),
            _const_spec((1, QK_NORM_WIDTH)),
            _const_spec((QK_NORM_WIDTH, QK_NORM_WIDTH)),
        ],
        out_specs=[
            pl.BlockSpec((tm, DA_WIDTH), row),
            pl.BlockSpec((tm, akx_w), row),
            pl.BlockSpec((tm, DA_WIDTH), row),
            pl.BlockSpec((None, DA_WIDTH, tm), vt_tile),
            pl.BlockSpec((tm, GB_WIDTH), row),
            pl.BlockSpec((GB_KV, tm, GB_DIM), kv_rows),
            pl.BlockSpec((GB_KV, tm, GB_DIM), kv_rows),
            pl.BlockSpec((None, GB_KV_WIDTH, tm), vt_tile),
        ],
        out_shape=[
            jax.ShapeDtypeStruct((rows, DA_WIDTH), BF16),
            jax.ShapeDtypeStruct((rows, akx_w), BF16),
            jax.ShapeDtypeStruct((rows, DA_WIDTH), BF16),
            jax.ShapeDtypeStruct((rows // tm, DA_WIDTH, tm), BF16),
            jax.ShapeDtypeStruct((rows, GB_WIDTH), BF16),
            jax.ShapeDtypeStruct((GB_KV, rows, GB_DIM), BF16),
            jax.ShapeDtypeStruct((GB_KV, rows, GB_DIM), BF16),
            jax.ShapeDtypeStruct((rows // tm, GB_KV_WIDTH, tm), BF16),
        ],
        compiler_params=_cparams(("parallel",)),
        name="in_proj",
    )(h, g, w, wvt, cs, sn, kpos, gqk, bd)


def _softmax_step(s, v, m_ref, l_ref, acc_ref, j):
    m_prev = m_ref[j]
    m_new = jnp.maximum(m_prev, jnp.max(s, axis=1, keepdims=True))
    alpha = jnp.exp2(m_prev - m_new)
    p = jnp.exp2(s - m_new)
    l_ref[j] = alpha * l_ref[j] + jnp.sum(p, axis=1, keepdims=True)
    acc_ref[j] = alpha * acc_ref[j] + _dot(p.astype(BF16), v)
    m_ref[j] = m_new


def _attn_a_kernel(nslope_ref, lam_ref, q_ref, k_ref, v_ref, mk_ref, mv_ref, sg_ref, oin_ref,
                   o_ref, m_ref, l_ref, acc_ref, *, tq, tk, nk, q_is_meta, out_scale):
    del oin_ref
    nslope = nslope_ref[pl.program_id(1)]
    q = q_ref[...]
    qs = (q[:, :DA_DIM], q[:, DA_DIM:])
    row = lax.broadcasted_iota(jnp.int32, (tq, 1), 0)
    if q_is_meta:
        qpos = row.astype(F32)
    else:
        qpos = (row + (pl.program_id(2) * tq + N_META)).astype(F32)
    m_ref[...] = jnp.full(m_ref.shape, NEG_INF, F32)
    l_ref[...] = jnp.zeros(l_ref.shape, F32)
    acc_ref[...] = jnp.zeros(acc_ref.shape, F32)

    def tile(kb, vb, kpos):
        bias = jnp.abs(qpos - kpos) * nslope
        for j in range(2):
            s = _dot_nt(qs[j], kb[:, j * KX_W:j * KX_W + DA_DIM]) + bias
            _softmax_step(s, vb, m_ref, l_ref, acc_ref, j)

    def body(i, carry):
        k0 = pl.multiple_of(i * tk, tk)
        col = lax.broadcasted_iota(jnp.int32, (1, tk), 1)
        tile(k_ref[pl.ds(k0, tk), :], v_ref[pl.ds(k0, tk), :], (col + (k0 + N_META)).astype(F32))
        return carry

    lax.fori_loop(0, nk, body, 0)
    tile(mk_ref[...], mv_ref[...], lax.broadcasted_iota(jnp.int32, (1, N_META), 1).astype(F32))
    o = acc_ref[0] / l_ref[0] - lam_ref[0] * (acc_ref[1] / l_ref[1])
    o_ref[...] = (_rms(o, sg_ref[...]) * out_scale).astype(o_ref.dtype)


def _attn_b_kernel(q_ref, k_ref, v_ref, mk_ref, mv_ref, oin_ref,
                   o_ref, qs_ref, m_ref, l_ref, acc_ref, *, tq, tk, nk):
    del oin_ref
    for g in range(GB_GROUP):
        qs_ref[g * tq:(g + 1) * tq, :] = q_ref[:, g * GB_DIM:(g + 1) * GB_DIM]
    qs = qs_ref[...]
    m_ref[...] = jnp.full(m_ref.shape, NEG_INF, F32)
    l_ref[...] = jnp.zeros(l_ref.shape, F32)
    acc_ref[...] = jnp.zeros(acc_ref.shape, F32)

    def tile(kb, vb):
        _softmax_step(_dot_nt(qs, kb), vb, m_ref, l_ref, acc_ref, 0)

    def body(i, carry):
        k0 = pl.multiple_of(i * tk, tk)
        tile(k_ref[pl.ds(k0, tk), :], v_ref[pl.ds(k0, tk), :])
        return carry

    lax.fori_loop(0, nk, body, 0)
    tile(mk_ref[...], mv_ref[...])
    o = acc_ref[0] / l_ref[0]
    for g in range(GB_GROUP):
        o_ref[:, g * GB_DIM:(g + 1) * GB_DIM] = o[g * tq:(g + 1) * tq].astype(o_ref.dtype)


def _attention_meta(o_buf, aq, akx, av, bq, bk, bv, nslope, lam, subln_g, out_scale, *,
                    n_seq, seq_len, row0, seq0, n_real):
    assert row0 % seq_len == 0 and seq_len % META_TK == 0
    nk = seq_len // META_TK
    kv_blk0 = row0 // seq_len
    meta_blk0 = n_real // N_META + seq0
    any_spec = pl.BlockSpec(memory_space=pl.ANY)
    smem = pl.BlockSpec(memory_space=pltpu.SMEM)

    tq = N_META
    nq = 1
    q_map = lambda b, h, i: (meta_blk0 + b, h)
    kv_map = lambda b, h, i: (kv_blk0 + b, h)
    mkv_map = lambda b, h, i: (meta_blk0 + b, h)
    w = 2 * DA_DIM
    o_buf = pl.pallas_call(
        functools.partial(_attn_a_kernel, tq=tq, tk=META_TK, nk=nk, q_is_meta=True, out_scale=out_scale),
        grid=(n_seq, DA_HEADS, nq),
        in_specs=[
            smem, smem,
            pl.BlockSpec((tq, w), q_map),
            pl.BlockSpec((seq_len, 2 * KX_W), kv_map),
            pl.BlockSpec((seq_len, w), kv_map),
            pl.BlockSpec((N_META, 2 * KX_W), mkv_map),
            pl.BlockSpec((N_META, w), mkv_map),
            pl.BlockSpec((1, w), lambda b, h, i: (0, 0)),
            any_spec,
        ],
        out_specs=pl.BlockSpec((tq, w), q_map),
        out_shape=jax.ShapeDtypeStruct(o_buf.shape, o_buf.dtype),
        scratch_shapes=[
            pltpu.VMEM((2, tq, 1), F32),
            pltpu.VMEM((2, tq, 1), F32),
            pltpu.VMEM((2, tq, w), F32),
        ],
        input_output_aliases={8: 0},
        compiler_params=_cparams(("parallel", "parallel", "parallel")),
        name="attn_diff_meta",
    )(nslope, lam, aq, akx, av, akx, av, subln_g, o_buf)

    qw = GB_GROUP * GB_DIM
    ocol0 = DA_WIDTH // qw
    q_map = lambda b, c, i: (meta_blk0 + b, c)
    o_map = lambda b, c, i: (meta_blk0 + b, ocol0 + c)
    kv_map = lambda b, c, i: (c, kv_blk0 + b, 0)
    mkv_map = lambda b, c, i: (c, meta_blk0 + b, 0)
    o_buf = pl.pallas_call(
        functools.partial(_attn_b_kernel, tq=tq, tk=META_TK, nk=nk),
        grid=(n_seq, GB_KV, nq),
        in_specs=[
            pl.BlockSpec((tq, qw), q_map),
            pl.BlockSpec((None, seq_len, GB_DIM), kv_map),
            pl.BlockSpec((None, seq_len, GB_DIM), kv_map),
            pl.BlockSpec((None, N_META, GB_DIM), mkv_map),
            pl.BlockSpec((None, N_META, GB_DIM), mkv_map),
            any_spec,
        ],
        out_specs=pl.BlockSpec((tq, qw), o_map),
        out_shape=jax.ShapeDtypeStruct(o_buf.shape, o_buf.dtype),
        scratch_shapes=[
            pltpu.VMEM((GB_GROUP * tq, GB_DIM), BF16),
            pltpu.VMEM((1, GB_GROUP * tq, 1), F32),
            pltpu.VMEM((1, GB_GROUP * tq, 1), F32),
            pltpu.VMEM((1, GB_GROUP * tq, GB_DIM), F32),
        ],
        input_output_aliases={5: 0},
        compiler_params=_cparams(("parallel", "parallel", "parallel")),
        name="attn_gqa_meta",
    )(bq, bk, bv, bk, bv, o_buf)
    return o_buf


def _own_meta_rows(shape, seq):
    r = lax.broadcasted_iota(jnp.int32, shape, 0)
    return (r >= seq * N_META) & (r < (seq + 1) * N_META)


def _split3(x):
    hi = x.astype(BF16).astype(F32)
    r = x - hi
    mid = r.astype(BF16).astype(F32)
    return hi, mid, (r - mid).astype(BF16).astype(F32)


def _attn_a_t_kernel(nslope_ref, lam_ref, q_ref, k_ref, vt_ref, mk_ref, mvt_ref, sg_ref, oin_ref,
                     o_ref, qx_ref, s_ref, cmax_ref, p_ref, alpha_ref, m_ref, acc_ref, *,
                     t, nk, seq0, out_scale):
    del oin_ref
    qi = pl.program_id(2)
    nslope = nslope_ref[pl.program_id(1)]
    q = q_ref[...]
    qpos = (lax.broadcasted_iota(jnp.int32, (t, 1), 0) + (qi * t + N_META)).astype(F32)
    lane = lax.broadcasted_iota(jnp.int32, (t, POS_W), 1)
    c1, c2, c3 = _split3(jnp.full((t, POS_W), -nslope, F32))
    b1, b2, b3 = _split3(jnp.broadcast_to(nslope * qpos, (t, POS_W)))
    coef = jnp.where((lane == 0) | (lane == 3), c1, jnp.where((lane == 1) | (lane == 4), c2,
           jnp.where((lane == 2) | (lane == 5), c3, jnp.where(lane == 6, b1, jnp.where(lane == 7, b2,
           jnp.where(lane == 8, b3, 0.0))))))
    for v, sign in enumerate((1.0, 0.0, -1.0)):
        ext = (coef * sign).astype(BF16)
        for j in range(2):
            qx_ref[2 * v + j] = jnp.concatenate([q[:, j * DA_DIM:(j + 1) * DA_DIM], ext], axis=1)
    m_ref[...] = jnp.full(m_ref.shape, NEG_INF, F32)
    acc_ref[...] = jnp.zeros(acc_ref.shape, F32)

    def scores(kb, v, slot, bias=None, mask=None):
        for j in range(2):
            s = _dot_nt(kb[:, j * KX_W:(j + 1) * KX_W], qx_ref[2 * v + j])
            if bias is not None:
                s = s + bias
            if mask is not None:
                s = jnp.where(mask, s, NEG_INF)
            s_ref[slot, j] = s
            cmax_ref[slot, j] = jnp.max(s, axis=0, keepdims=True)

    def softmax(slot):
        for j in range(2):
            m_prev = m_ref[j]
            m_new = jnp.maximum(m_prev, cmax_ref[slot, j])
            alpha = jnp.exp2(m_prev - m_new)
            p = jnp.exp2(s_ref[slot, j] - m_new)
            m_ref[j] = m_new
            alpha_ref[slot, j] = alpha
            p_ref[slot, j] = p.astype(BF16)

    def accumulate(vtb, slot):
        vtx = jnp.concatenate([vtb, jnp.ones((ONES_ROWS, vtb.shape[1]), BF16)], axis=0)
        for j in range(2):
            acc_ref[j] = alpha_ref[slot, j] * acc_ref[j] + _dot(vtx, p_ref[slot, j])

    def other(n):
        return jnp.where(n >= qi, n + 1, n)

    def other_scores(n, slot):
        i = other(n)
        scores(k_ref[pl.ds(pl.multiple_of(i * t, t), t), :], jnp.where(i < qi, 0, 2), slot)

    kk = lax.broadcasted_iota(jnp.int32, (t, t), 0)
    qq = lax.broadcasted_iota(jnp.int32, (t, t), 1)
    scores(k_ref[pl.ds(pl.multiple_of(qi * t, t), t), :], 1, 0, bias=jnp.abs(kk - qq).astype(F32) * nslope)
    other_scores(0, 1)
    softmax(0)

    def pair(jj, carry):
        p = 2 * jj + 1
        accumulate(vt_ref[jnp.where(p == 1, qi, other(p - 2))], 0)
        other_scores(p, 0)
        softmax(1)
        accumulate(vt_ref[other(p - 1)], 1)
        other_scores(p + 1, 1)
        softmax(0)
        return carry

    lax.fori_loop(0, (nk - 2) // 2, pair, 0, unroll=5)
    accumulate(vt_ref[qi if nk == 2 else other(nk - 3)], 0)
    scores(mk_ref[...], 0, 0, mask=_own_meta_rows((t, t), seq0 + pl.program_id(0)))
    softmax(1)
    accumulate(vt_ref[other(nk - 2)], 1)
    softmax(0)
    accumulate(mvt_ref[...], 0)

    dv = 2 * DA_DIM
    a0, a1 = acc_ref[0], acc_ref[1]
    ot = a0[:dv] / a0[dv:dv + 1] - lam_ref[0] * (a1[:dv] / a1[dv:dv + 1])
    o_ref[...] = (_rms(ot.T, sg_ref[...]) * out_scale).astype(o_ref.dtype)


def _attn_b_t_kernel(q_ref, k_ref, vt_ref, mk_ref, mvt_ref, oin_ref,
                     o_ref, qs_ref, s0_ref, s1_ref, c0_ref, c1_ref, p0_ref, p1_ref, a0_ref, a1_ref,
                     m_ref, acc_ref, *, tq, t, nk, seq0):
    del oin_ref
    assert nk % 2 == 0
    s_ref, cmax_ref, p_ref, alpha_ref = (s0_ref, s1_ref), (c0_ref, c1_ref), (p0_ref, p1_ref), (a0_ref, a1_ref)
    for g in range(GB_GROUP):
        qs_ref[g * tq:(g + 1) * tq, :] = q_ref[:, g * GB_DIM:(g + 1) * GB_DIM]
    qs = qs_ref[...]
    nq = GB_GROUP * tq
    m_ref[...] = jnp.full(m_ref.shape, NEG_INF, F32)
    acc_ref[...] = jnp.zeros(acc_ref.shape, F32)

    def scores(kb, slot, mask=None):
        s = _dot_nt(kb, qs)
        if mask is not None:
            s = jnp.where(mask, s, NEG_INF)
        s_ref[slot][...] = s
        cmax_ref[slot][...] = jnp.max(s, axis=0, keepdims=True)

    def seq_scores(i, slot):
        scores(k_ref[pl.ds(pl.multiple_of(i * t, t), t), :], slot)

    def softmax(slot):
        m_prev = m_ref[...]
        m_new = jnp.maximum(m_prev, cmax_ref[slot][...])
        alpha = jnp.exp2(m_prev - m_new)
        p = jnp.exp2(s_ref[slot][...] - m_new)
        m_ref[...] = m_new
        alpha_ref[slot][...] = alpha
        p_ref[slot][...] = p.astype(BF16)

    def accumulate(vtb, slot):
        vtx = jnp.concatenate([vtb, jnp.ones((ONES_ROWS, vtb.shape[1]), BF16)], axis=0)
        acc_ref[...] = alpha_ref[slot][...] * acc_ref[...] + _dot(vtx, p_ref[slot][...])

    seq_scores(0, 0)
    seq_scores(1, 1)
    softmax(0)

    def pair(jj, carry):
        p = 2 * jj + 1
        accumulate(vt_ref[p - 1], 0)
        seq_scores(p + 1, 0)
        softmax(1)
        accumulate(vt_ref[p], 1)
        seq_scores(p + 2, 1)
        softmax(0)
        return carry

    lax.fori_loop(0, (nk - 2) // 2, pair, 0, unroll=5)
    accumulate(vt_ref[nk - 2], 0)
    scores(mk_ref[...], 0, mask=_own_meta_rows((t, nq), seq0 + pl.program_id(0)))
    softmax(1)
    accumulate(vt_ref[nk - 1], 1)
    softmax(0)
    accumulate(mvt_ref[...], 0)
    acc = acc_ref[...]
    ot = acc[:GB_DIM] / acc[GB_DIM:GB_DIM + 1]
    for g in range(0, GB_GROUP, 2):
        pair = jnp.concatenate([ot[:, g * tq:(g + 1) * tq], ot[:, (g + 1) * tq:(g + 2) * tq]], axis=0)
        o_ref[:, g * GB_DIM:(g + 2) * GB_DIM] = pair.T.astype(o_ref.dtype)


def _attention_main(o_buf, aq, akx, avt, bq, bk, bvt, nslope, lam, subln_g, out_scale, *,
                    n_seq, seq_len, row0, seq0, n_real):
    t = ATTN_TILE
    assert row0 % seq_len == 0 and seq_len % t == 0 and n_real % t == 0
    nk = seq_len // t
    kv_blk0 = row0 // seq_len
    meta_tile = n_real // t
    any_spec = pl.BlockSpec(memory_space=pl.ANY)
    smem = pl.BlockSpec(memory_space=pltpu.SMEM)

    q_blk0 = row0 // t
    q_map = lambda b, h, i: (q_blk0 + b * nk + i, h)
    w = 2 * DA_DIM
    o_buf = pl.pallas_call(
        functools.partial(_attn_a_t_kernel, t=t, nk=nk, seq0=seq0, out_scale=out_scale),
        grid=(n_seq, DA_HEADS, nk),
        in_specs=[
            smem, smem,
            pl.BlockSpec((t, w), q_map),
            pl.BlockSpec((seq_len, 2 * KX_W), lambda b, h, i: (kv_blk0 + b, h)),
            pl.BlockSpec((nk, w, t), lambda b, h, i: (kv_blk0 + b, h, 0)),
            pl.BlockSpec((t, 2 * KX_W), lambda b, h, i: (meta_tile, h)),
            pl.BlockSpec((None, w, t), lambda b, h, i: (meta_tile, h, 0)),
            pl.BlockSpec((1, w), lambda b, h, i: (0, 0)),
            any_spec,
        ],
        out_specs=pl.BlockSpec((t, w), q_map),
        out_shape=jax.ShapeDtypeStruct(o_buf.shape, o_buf.dtype),
        scratch_shapes=[
            pltpu.VMEM((6, t, KX_W), BF16),
            pltpu.VMEM((2, 2, t, t), F32),
            pltpu.VMEM((2, 2, 1, t), F32),
            pltpu.VMEM((2, 2, t, t), BF16),
            pltpu.VMEM((2, 2, 1, t), F32),
            pltpu.VMEM((2, 1, t), F32),
            pltpu.VMEM((2, w + ONES_ROWS, t), F32),
        ],
        input_output_aliases={8: 0},
        compiler_params=_cparams(("parallel", "parallel", "parallel")),
        name="attn_diff",
    )(nslope, lam, aq, akx, avt, akx, avt, subln_g, o_buf)

    tq = ATTN_B_TQ
    nq = seq_len // tq
    qw = GB_GROUP * GB_DIM
    ocol0 = DA_WIDTH // qw
    q_blk0 = row0 // tq
    o_buf = pl.pallas_call(
        functools.partial(_attn_b_t_kernel, tq=tq, t=t, nk=nk, seq0=seq0),
        grid=(n_seq, GB_KV, nq),
        in_specs=[
            pl.BlockSpec((tq, qw), lambda b, c, i: (q_blk0 + b * nq + i, c)),
            pl.BlockSpec((None, seq_len, GB_DIM), lambda b, c, i: (c, kv_blk0 + b, 0)),
            pl.BlockSpec((nk, GB_DIM, t), lambda b, c, i: (kv_blk0 + b, c, 0)),
            pl.BlockSpec((None, t, GB_DIM), lambda b, c, i: (c, meta_tile, 0)),
            pl.BlockSpec((None, GB_DIM, t), lambda b, c, i: (meta_tile, c, 0)),
            any_spec,
        ],
        out_specs=pl.BlockSpec((tq, qw), lambda b, c, i: (q_blk0 + b * nq + i, ocol0 + c)),
        out_shape=jax.ShapeDtypeStruct(o_buf.shape, o_buf.dtype),
        scratch_shapes=[
            pltpu.VMEM((GB_GROUP * tq, GB_DIM), BF16),
            pltpu.VMEM((t, GB_GROUP * tq), F32), pltpu.VMEM((t, GB_GROUP * tq), F32),
            pltpu.VMEM((1, GB_GROUP * tq), F32), pltpu.VMEM((1, GB_GROUP * tq), F32),
            pltpu.VMEM((t, GB_GROUP * tq), BF16), pltpu.VMEM((t, GB_GROUP * tq), BF16),
            pltpu.VMEM((1, GB_GROUP * tq), F32), pltpu.VMEM((1, GB_GROUP * tq), F32),
            pltpu.VMEM((1, GB_GROUP * tq), F32),
            pltpu.VMEM((GB_DIM + ONES_ROWS, GB_GROUP * tq), F32),
        ],
        input_output_aliases={5: 0},
        compiler_params=_cparams(("parallel", "parallel", "parallel")),
        name="attn_gqa",
    )(bq, bk, bvt, bk, bvt, o_buf)
    return o_buf


def _swiglu(u, wg, wu, wd):
    a = _dot(u, wg)
    b = _dot(u, wu)
    hid = (a * jax.nn.sigmoid(a) * b).astype(BF16)
    return _dot(hid, wd)


def _out_ffn_kernel(o_ref, h_ref, wo_ref, g_ref, wg_ref, wu_ref, wd_ref, out_ref):
    h1 = h_ref[...] + _dot(o_ref[...], wo_ref[...])
    u = _rms(h1, g_ref[...]).astype(BF16)
    out_ref[...] = h1 + _swiglu(u, wg_ref[...], wu_ref[...], wd_ref[...])


def _out_ffn(o, h, wo, g, wg, wu, wd):
    rows, d = h.shape
    dff = wg.shape[1]
    tm = ROW_TILE
    row = lambda i: (i, 0)
    return pl.pallas_call(
        _out_ffn_kernel,
        grid=(rows // tm,),
        in_specs=[
            pl.BlockSpec((tm, d), row),
            pl.BlockSpec((tm, d), row),
            _const_spec((d, d)),
            _const_spec((1, d)),
            _const_spec((d, dff)),
            _const_spec((d, dff)),
            _const_spec((dff, d)),
        ],
        out_specs=pl.BlockSpec((tm, d), row),
        out_shape=jax.ShapeDtypeStruct((rows, d), F32),
        compiler_params=_cparams(("parallel",)),
        name="out_proj_ffn",
    )(o, h, wo, g, wg, wu, wd)


def _out_router_kernel(o_ref, h_ref, wo_ref, g_ref, wrh_ref, wrl_ref, h1_ref, u_ref, ri_ref):
    h1 = h_ref[...] + _dot(o_ref[...], wo_ref[...])
    h1_ref[...] = h1
    u = _rms(h1, g_ref[...])
    u_ref[...] = u
    uh = u.astype(BF16)
    ul = (u - uh.astype(F32)).astype(BF16)
    wrh = wrh_ref[...]
    logits = _dot(uh, wrh) + _dot(ul, wrh) + _dot(uh, wrl_ref[...])
    lane = lax.broadcasted_iota(jnp.int32, logits.shape, 1)
    lg = jnp.where(lane < N_EXPERTS, logits, NEG_INF)
    v1 = jnp.max(lg, axis=1, keepdims=True)
    i1 = jnp.min(jnp.where(lg == v1, lane, LANES), axis=1, keepdims=True)
    lg2 = jnp.where(lane == i1, NEG_INF, lg)
    v2 = jnp.max(lg2, axis=1, keepdims=True)
    i2 = jnp.min(jnp.where(lg2 == v2, lane, LANES), axis=1, keepdims=True)
    e = jnp.exp(v2 - v1)
    g1 = 1.0 / (1.0 + e)
    g2 = e * g1
    ri_ref[...] = jnp.where(lane == 0, i1.astype(F32),
                            jnp.where(lane == 1, i2.astype(F32),
                                      jnp.where(lane == 2, g1, jnp.where(lane == 3, g2, 0.0))))


def _out_router(o, h, wo, g, wrh, wrl, n_rows):
    d = h.shape[1]
    tm = ROW_TILE
    row = lambda i: (i, 0)
    return pl.pallas_call(
        _out_router_kernel,
        grid=(n_rows // tm,),
        in_specs=[
            pl.BlockSpec((tm, d), row),
            pl.BlockSpec((tm, d), row),
            _const_spec((d, d)),
            _const_spec((1, d)),
            _const_spec((d, LANES)),
            _const_spec((d, LANES)),
        ],
        out_specs=[
            pl.BlockSpec((tm, d), row),
            pl.BlockSpec((tm, d), row),
            pl.BlockSpec((tm, LANES), row),
        ],
        out_shape=[
            jax.ShapeDtypeStruct((n_rows, d), F32),
            jax.ShapeDtypeStruct((n_rows, d), F32),
            jax.ShapeDtypeStruct((n_rows, LANES), F32),
        ],
        compiler_params=_cparams(("parallel",)),
        name="out_proj_router",
    )(o, h, wo, g, wrh, wrl)


def _dispatch_kernel(slot_ref, u_ref, xs_in, xs_hbm, sem, *, tm):
    del xs_in

    def copy(t, k):
        return pltpu.make_async_copy(u_ref.at[pl.ds(t, 1)], xs_hbm.at[pl.ds(slot_ref[0, k, t], 1)], sem)

    def issue(t, carry):
        for k in range(TOP_K):
            copy(t, k).start()
        return carry

    def drain(t, carry):
        for k in range(TOP_K):
            copy(t, k).wait()
        return carry

    lax.fori_loop(0, tm, issue, 0)
    lax.fori_loop(0, tm, drain, 0)


def _dispatch(slots, u, xs_init, tm):
    n_rows, d = u.shape
    return pl.pallas_call(
        functools.partial(_dispatch_kernel, tm=tm),
        grid=(n_rows // tm,),
        in_specs=[
            pl.BlockSpec((1, TOP_K, tm), lambda i: (i, 0, 0), memory_space=pltpu.SMEM),
            pl.BlockSpec((tm, d), lambda i: (i, 0)),
            pl.BlockSpec(memory_space=pl.ANY),
        ],
        out_specs=pl.BlockSpec(memory_space=pl.ANY),
        out_shape=jax.ShapeDtypeStruct(xs_init.shape, xs_init.dtype),
        scratch_shapes=[pltpu.SemaphoreType.DMA(())],
        input_output_aliases={2: 0},
        compiler_params=pltpu.CompilerParams(dimension_semantics=("arbitrary",), has_side_effects=True,
                                             vmem_limit_bytes=VMEM_LIMIT),
        name="moe_dispatch",
    )(slots, u, xs_init)


def _moe_ffn_kernel(te_ref, nu_ref, x_ref, wg_ref, wu_ref, wd_ref, y_ref):
    del te_ref
    used = pl.program_id(0) < nu_ref[0]

    @pl.when(used)
    def _():
        y_ref[...] = _swiglu(x_ref[...].astype(BF16), wg_ref[0], wu_ref[0], wd_ref[0])

    @pl.when(jnp.logical_not(used))
    def _():
        y_ref[...] = jnp.zeros(y_ref.shape, y_ref.dtype)


def _moe_ffn(tile_expert, n_used, xs, wg, wu, wd, tm):
    n_slots, d = xs.shape
    dff = wg.shape[2]
    x_map = lambda i, te, nu: (jnp.minimum(i, nu[0] - 1), 0)
    w_map = lambda i, te, nu: (te[i], 0, 0)
    grid_spec = pltpu.PrefetchScalarGridSpec(
        num_scalar_prefetch=2,
        grid=(n_slots // tm,),
        in_specs=[
            pl.BlockSpec((tm, d), x_map),
            pl.BlockSpec((1, d, dff), w_map, pipeline_mode=pl.Buffered(1)),
            pl.BlockSpec((1, d, dff), w_map, pipeline_mode=pl.Buffered(1)),
            pl.BlockSpec((1, dff, d), w_map, pipeline_mode=pl.Buffered(1)),
        ],
        out_specs=pl.BlockSpec((tm, d), lambda i, te, nu: (i, 0)),
    )
    return pl.pallas_call(
        _moe_ffn_kernel,
        grid_spec=grid_spec,
        out_shape=jax.ShapeDtypeStruct((n_slots, d), F32),
        compiler_params=_cparams(("arbitrary",)),
        name="moe_ffn",
    )(tile_expert, n_used, xs, wg, wu, wd)


def _combine_kernel(slot_ref, h1_ref, ri_ref, g_ref, ys_hbm, out_ref, buf_ref, sem, *, tm, final_norm):
    def copy(t, k):
        return pltpu.make_async_copy(ys_hbm.at[pl.ds(slot_ref[0, k, t], 1)], buf_ref.at[k, pl.ds(t, 1)], sem)

    def issue(t, carry):
        for k in range(TOP_K):
            copy(t, k).start()
        return carry

    def drain(t, carry):
        for k in range(TOP_K):
            copy(t, k).wait()
        return carry

    lax.fori_loop(0, tm, issue, 0)
    lax.fori_loop(0, tm, drain, 0)
    ri = ri_ref[...]
    h2 = h1_ref[...] + ri[:, 2:3] * buf_ref[0] + ri[:, 3:4] * buf_ref[1]
    out_ref[...] = _rms(h2, g_ref[...]) if final_norm else h2


def _combine(slots, h1, ri, g, ys, tm, final_norm):
    n_rows, d = h1.shape
    row = lambda i: (i, 0)
    return pl.pallas_call(
        functools.partial(_combine_kernel, tm=tm, final_norm=final_norm),
        grid=(n_rows // tm,),
        in_specs=[
            pl.BlockSpec((1, TOP_K, tm), lambda i: (i, 0, 0), memory_space=pltpu.SMEM),
            pl.BlockSpec((tm, d), row),
            pl.BlockSpec((tm, LANES), row),
            _const_spec((1, d)),
            pl.BlockSpec(memory_space=pl.ANY),
        ],
        out_specs=pl.BlockSpec((tm, d), row),
        out_shape=jax.ShapeDtypeStruct((n_rows, d), F32),
        scratch_shapes=[pltpu.VMEM((TOP_K, tm, d), F32), pltpu.SemaphoreType.DMA(())],
        compiler_params=_cparams(("arbitrary",)),
        name="moe_combine_norm",
    )(slots, h1, ri, g, ys)


def _final_norm_kernel(h_ref, g_ref, out_ref):
    out_ref[...] = _rms(h_ref[...], g_ref[...])


def _final_norm(h, g, n_rows):
    d = h.shape[1]
    tm = ROW_TILE
    row = lambda i: (i, 0)
    return pl.pallas_call(
        _final_norm_kernel,
        grid=(n_rows // tm,),
        in_specs=[pl.BlockSpec((tm, d), row), _const_spec((1, d))],
        out_specs=pl.BlockSpec((tm, d), row),
        out_shape=jax.ShapeDtypeStruct((n_rows, d), F32),
        compiler_params=_cparams(("parallel",)),
        name="final_norm",
    )(h, g)


def _rope_tables(groups, pad_rows):
    n_freq = GB_DIM // 4
    inv = ROPE_THETA ** (-jnp.arange(n_freq, dtype=F32) * 2.0 / (GB_DIM // 2))
    rows, cols = [], []
    for n_seq, seq_len in groups:
        t = jnp.arange(seq_len, dtype=jnp.int32)
        rows.append(jnp.tile((t // GRID_W).astype(F32), n_seq))
        cols.append(jnp.tile((t % GRID_W).astype(F32), n_seq))
    n_total_seq = sum(n for n, _ in groups)
    meta = jnp.tile(jnp.arange(N_META, dtype=F32) - N_META, n_total_seq)
    pad = jnp.zeros((pad_rows,), F32)
    r = jnp.concatenate(rows + [meta, pad])
    c = jnp.concatenate(cols + [meta, pad])
    ang = jnp.concatenate([r[:, None] * inv, c[:, None] * inv], axis=-1)
    cos, sin = jnp.cos(ang), jnp.sin(ang)
    reps = LANES // GB_DIM
    return jnp.tile(jnp.concatenate([cos, cos], axis=-1), (1, reps)), jnp.tile(jnp.concatenate([-sin, sin], axis=-1), (1, reps))


def _key_position_columns(groups, pad_rows):
    n_total_seq = sum(n for n, _ in groups)
    pos = jnp.concatenate(
        [jnp.tile(jnp.arange(s, dtype=jnp.int32) + N_META, n) for n, s in groups]
        + [jnp.tile(jnp.arange(N_META, dtype=jnp.int32), n_total_seq), jnp.zeros((pad_rows,), jnp.int32)])
    hi = ((pos // LANES) * LANES).astype(F32)[:, None]
    lo = (pos % LANES).astype(F32)[:, None]
    col = jnp.arange(POS_W)[None, :]
    table = jnp.where(col < 3, hi, jnp.where(col < 6, lo, jnp.where(col < 9, 1.0, 0.0)))
    return table.astype(BF16)


def _half_split_perm():
    return jnp.concatenate([jnp.arange(0, GB_DIM, 2), jnp.arange(1, GB_DIM, 2)])


def _route(ri, tm):
    n = ri.shape[0]
    experts = ri[:, :TOP_K].astype(jnp.int32).T.reshape(-1)
    onehot = (experts[:, None] == jnp.arange(N_EXPERTS, dtype=jnp.int32)[None, :]).astype(jnp.int32)
    incl = jnp.cumsum(onehot, axis=0)
    counts = incl[-1]
    pos = jnp.sum((incl - onehot) * onehot, axis=1)
    padded = ((counts + tm - 1) // tm) * tm
    ends = jnp.cumsum(padded)
    starts = ends - padded
    slot = jnp.sum(onehot * starts[None, :], axis=1) + pos
    slots = slot.reshape(TOP_K, n // tm, tm).transpose(1, 0, 2)
    n_slots = TOP_K * n + N_EXPERTS * tm
    tile_start = jnp.arange(n_slots // tm, dtype=jnp.int32) * tm
    tile_expert = jnp.sum((tile_start[:, None] >= ends[None, :]).astype(jnp.int32), axis=1)
    n_used = (ends[-1] // tm).astype(jnp.int32)
    last_used_expert = jnp.sum((jnp.maximum(ends[-1] - tm, 0) >= ends).astype(jnp.int32))
    tile_expert = jnp.where(tile_start < ends[-1], tile_expert, last_used_expert).astype(jnp.int32)
    return slots.astype(jnp.int32), tile_expert, n_used.reshape(1), n_slots


def kernel(x_prompt, x_sample, meta_tokens, norm1_g, w_in, lambda_q1, lambda_k1, lambda_q2, lambda_k2, subln_g, q_norm_g, k_norm_g, w_out, norm2_g, ffn_w_gate, ffn_w_up, ffn_w_down, router_w, moe_w_gate, moe_w_up, moe_w_down, final_norm_g):
    depth = w_in.shape[0]
    d = x_prompt.shape[-1]
    groups = [(x_prompt.shape[0], x_prompt.shape[1]), (x_sample.shape[0], x_sample.shape[1])]
    n_seq_total = sum(n for n, _ in groups)
    n_real = sum(n * s for n, s in groups)
    assert n_real % ROW_TILE == 0
    meta_rows = n_seq_total * N_META
    meta_pad = -(-meta_rows // ROW_TILE) * ROW_TILE
    n_rows = n_real + meta_pad

    h = jnp.concatenate([
        x_prompt.reshape(-1, d), x_sample.reshape(-1, d),
        jnp.tile(meta_tokens.astype(F32), (n_seq_total, 1)),
        jnp.zeros((meta_pad - meta_rows, d), F32)], axis=0)

    cs, sn = _rope_tables(groups, meta_pad - meta_rows)
    kpos = _key_position_columns(groups, meta_pad - meta_rows)
    perm = _half_split_perm()
    c0 = 3 * DA_WIDTH
    col_perm = jnp.concatenate(
        [jnp.arange(c0)]
        + [c0 + hd * GB_DIM + perm for hd in range(GB_HEADS + GB_KV)]
        + [jnp.arange(c0 + QK_NORM_WIDTH, w_in.shape[2])])
    head_id = jnp.arange(QK_NORM_WIDTH) // GB_DIM
    block_diag = (head_id[:, None] == head_id[None, :]).astype(BF16)
    slopes = (2.0 ** (-8.0 / DA_HEADS)) ** jnp.arange(1, DA_HEADS + 1, dtype=F32)
    nslope = -slopes * math.log2(math.e)

    out = None
    for layer in range(depth):
        last = layer == depth - 1
        lambda_init = 0.8 - 0.6 * math.exp(-0.3 * layer)
        lam = (jnp.exp(jnp.sum(lambda_q1[layer].astype(F32) * lambda_k1[layer].astype(F32)))
               - jnp.exp(jnp.sum(lambda_q2[layer].astype(F32) * lambda_k2[layer].astype(F32))) + lambda_init)
        gqk = jnp.concatenate([jnp.tile(q_norm_g[layer][perm], GB_HEADS), jnp.tile(k_norm_g[layer][perm], GB_KV)])
        w_l = w_in[layer]
        wvt = jnp.concatenate([w_l[:, 2 * DA_WIDTH:3 * DA_WIDTH], w_l[:, c0 + QK_NORM_WIDTH:]], axis=1).T
        aq, akx, av, avt, bq, bk, bv, bvt = _in_proj(
            h, norm1_g[layer][None], w_l[:, col_perm].astype(BF16), wvt.astype(BF16), cs, sn, kpos,
            gqk[None].astype(F32), block_diag)

        o = jnp.zeros((n_rows, DA_WIDTH + GB_WIDTH), BF16)
        row0, seq0 = 0, 0
        for n_seq, seq_len in groups:
            where = dict(n_seq=n_seq, seq_len=seq_len, row0=row0, seq0=seq0, n_real=n_real)
            sg = subln_g[layer][None].astype(F32)
            o = _attention_main(o, aq, akx, avt, bq, bk, bvt, nslope, lam.reshape(1), sg, 1.0 - lambda_init, **where)
            if not last:
                o = _attention_meta(o, aq, akx, av, bq, bk, bv, nslope, lam.reshape(1), sg, 1.0 - lambda_init, **where)
            row0 += n_seq * seq_len
            seq0 += n_seq

        wo = w_out[layer].astype(BF16)
        g2 = norm2_g[layer][None]
        if layer % 2 == 0:
            i = layer // 2
            if last:
                h = _out_ffn(o[:n_real], h[:n_real], wo, g2, ffn_w_gate[i].astype(BF16), ffn_w_up[i].astype(BF16),
                             ffn_w_down[i].astype(BF16))
                out = _final_norm(h, final_norm_g[None], n_real)
            else:
                h = _out_ffn(o, h, wo, g2, ffn_w_gate[i].astype(BF16), ffn_w_up[i].astype(BF16),
                             ffn_w_down[i].astype(BF16))
        else:
            i = layer // 2
            n_tok = n_real if last else n_rows
            wr = jnp.zeros((d, LANES), F32).at[:, :N_EXPERTS].set(router_w[i])
            wrh = wr.astype(BF16)
            wrl = (wr - wrh.astype(F32)).astype(BF16)
            h1, u, ri = _out_router(o, h, wo, g2, wrh, wrl, n_tok)
            tm = ROW_TILE
            slots, tile_expert, n_used, n_slots = _route(ri, tm)
            xs = _dispatch(slots, u, jnp.zeros((n_slots, d), F32), tm)
            ys = _moe_ffn(tile_expert, n_used, xs, moe_w_gate[i].astype(BF16), moe_w_up[i].astype(BF16),
                          moe_w_down[i].astype(BF16), tm)
            res = _combine(slots, h1, ri, final_norm_g[None], ys, tm, final_norm=last)
            if last:
                out = res
            else:
                h = res

    y_prompt = out[:groups[0][0] * groups[0][1]].reshape(x_prompt.shape)
    y_sample = out[groups[0][0] * groups[0][1]:n_real].reshape(x_sample.shape)
    return (y_prompt, y_sample)
```

```python
import functools
import math

import jax
import jax.numpy as jnp
from jax import lax
from jax.experimental import pallas as pl
from jax.experimental.pallas import tpu as pltpu

F32 = jnp.float32
BF16 = jnp.bfloat16

N_META = 16
GRID_W = 64
DA_HEADS = 4
DA_DIM = 64
DA_WIDTH = DA_HEADS * 2 * DA_DIM
GB_HEADS = 8
GB_KV = 2
GB_GROUP = GB_HEADS // GB_KV
GB_DIM = 64
GB_WIDTH = GB_HEADS * GB_DIM
GB_KV_WIDTH = GB_KV * GB_DIM
ROPE_THETA = 10000.0
N_EXPERTS = 8
TOP_K = 2
EPS = 1e-6
QK_NORM_WIDTH = GB_WIDTH + GB_KV_WIDTH
LANES = 128
ROW_TILE = 256
ATTN_TILE = ROW_TILE
ATTN_B_TQ = 512
META_TK = 512
POS_W = 64
KX_W = DA_DIM + POS_W
ONES_ROWS = 16
VMEM_LIMIT = 56 * 1024 * 1024
Q_SCALE = (DA_DIM ** -0.5) * math.log2(math.e)
NEG_INF = float("-inf")


def _cparams(sem):
    return pltpu.CompilerParams(dimension_semantics=sem, vmem_limit_bytes=VMEM_LIMIT)


def _const_spec(shape):
    nd = len(shape)
    return pl.BlockSpec(shape, lambda *_: (0,) * nd, pipeline_mode=pl.Buffered(1))


def _rms(x, g):
    return x * lax.rsqrt(jnp.mean(x * x, axis=-1, keepdims=True) + EPS) * g


def _dot(a, b):
    return jnp.dot(a, b, preferred_element_type=F32)


def _dot_nt(a, b):
    return lax.dot_general(a, b, (((1,), (1,)), ((), ())), preferred_element_type=F32)


def _in_proj_kernel(h_ref, g_ref, w_ref, wvt_ref, cs_ref, sn_ref, kpos_ref, gqk_ref, bd_ref,
                    aq_ref, akx_ref, av_ref, avt_ref, bq_ref, bk_ref, bv_ref, bvt_ref):
    u = _rms(h_ref[...], g_ref[...]).astype(BF16)
    y = _dot(u, w_ref[...])
    aq_ref[...] = (y[:, :DA_WIDTH] * Q_SCALE).astype(BF16)
    kpos = kpos_ref[...]
    for b in range(2 * DA_HEADS):
        kmap = y[:, DA_WIDTH + b * DA_DIM:DA_WIDTH + (b + 1) * DA_DIM].astype(BF16)
        akx_ref[:, b * KX_W:(b + 1) * KX_W] = jnp.concatenate([kmap, kpos], axis=1)
    av_ref[...] = y[:, 2 * DA_WIDTH:3 * DA_WIDTH].astype(BF16)
    yt = _dot_nt(wvt_ref[...], u)
    avt_ref[...] = yt[:DA_WIDTH].astype(BF16)
    bvt_ref[...] = yt[DA_WIDTH:].astype(BF16)
    c0 = 3 * DA_WIDTH
    t = y[:, c0:c0 + QK_NORM_WIDTH]
    t2 = t * t
    hi = t2.astype(BF16)
    lo = (t2 - hi.astype(F32)).astype(BF16)
    bd = bd_ref[...]
    ss = _dot(hi, bd) + _dot(lo, bd)
    n = t * lax.rsqrt(ss * (1.0 / GB_DIM) + EPS) * gqk_ref[...]
    lane = lax.broadcasted_iota(jnp.int32, (t.shape[0], LANES), 1)
    first_half = (lane % GB_DIM) < (GB_DIM // 2)
    cs = cs_ref[...]
    sn = sn_ref[...]
    for c in range(QK_NORM_WIDTH // LANES):
        xc = n[:, c * LANES:(c + 1) * LANES]
        partner = jnp.where(first_half, pltpu.roll(xc, LANES - GB_DIM // 2, 1), pltpu.roll(xc, GB_DIM // 2, 1))
        r = xc * cs + partner * sn
        if c < GB_WIDTH // LANES:
            bq_ref[:, c * LANES:(c + 1) * LANES] = (r * Q_SCALE).astype(BF16)
        else:
            for kv in range(GB_KV):
                bk_ref[kv] = r[:, kv * GB_DIM:(kv + 1) * GB_DIM].astype(BF16)
    c1 = c0 + QK_NORM_WIDTH
    for kv in range(GB_KV):
        bv_ref[kv] = y[:, c1 + kv * GB_DIM:c1 + (kv + 1) * GB_DIM].astype(BF16)


def _in_proj(h, g, w, wvt, cs, sn, kpos, gqk, bd):
    rows, d = h.shape
    d_in = w.shape[1]
    tm = ROW_TILE
    assert tm == ATTN_TILE
    row = lambda i: (i, 0)
    kv_rows = lambda i: (0, i, 0)
    vt_tile = lambda i: (i, 0, 0)
    akx_w = 2 * DA_HEADS * KX_W
    return pl.pallas_call(
        _in_proj_kernel,
        grid=(rows // tm,),
        in_specs=[
            pl.BlockSpec((tm, d), row),
            _const_spec((1, d)),
            _const_spec((d, d_in)),
            _const_spec(wvt.shape),
            pl.BlockSpec((tm, LANES), row),
            pl.BlockSpec((tm, LANES), row),
            pl.BlockSpec((tm, POS_W), row),
            _const_spec((1, QK_NORM_WIDTH)),
            _const_spec((QK_NORM_WIDTH, QK_NORM_WIDTH)),
        ],
        out_specs=[
            pl.BlockSpec((tm, DA_WIDTH), row),
            pl.BlockSpec((tm, akx_w), row),
            pl.BlockSpec((tm, DA_WIDTH), row),
            pl.BlockSpec((None, DA_WIDTH, tm), vt_tile),
            pl.BlockSpec((tm, GB_WIDTH), row),
            pl.BlockSpec((GB_KV, tm, GB_DIM), kv_rows),
            pl.BlockSpec((GB_KV, tm, GB_DIM), kv_rows),
            pl.BlockSpec((None, GB_KV_WIDTH, tm), vt_tile),
        ],
        out_shape=[
            jax.ShapeDtypeStruct((rows, DA_WIDTH), BF16),
            jax.ShapeDtypeStruct((rows, akx_w), BF16),
            jax.ShapeDtypeStruct((rows, DA_WIDTH), BF16),
            jax.ShapeDtypeStruct((rows // tm, DA_WIDTH, tm), BF16),
            jax.ShapeDtypeStruct((rows, GB_WIDTH), BF16),
            jax.ShapeDtypeStruct((GB_KV, rows, GB_DIM), BF16),
            jax.ShapeDtypeStruct((GB_KV, rows, GB_DIM), BF16),
            jax.ShapeDtypeStruct((rows // tm, GB_KV_WIDTH, tm), BF16),
        ],
        compiler_params=_cparams(("parallel",)),
        name="in_proj",
    )(h, g, w, wvt, cs, sn, kpos, gqk, bd)


def _softmax_step(s, v, m_ref, l_ref, acc_ref, j):
    m_prev = m_ref[j]
    m_new = jnp.maximum(m_prev, jnp.max(s, axis=1, keepdims=True))
    alpha = jnp.exp2(m_prev - m_new)
    p = jnp.exp2(s - m_new)
    l_ref[j] = alpha * l_ref[j] + jnp.sum(p, axis=1, keepdims=True)
    acc_ref[j] = alpha * acc_ref[j] + _dot(p.astype(BF16), v)
    m_ref[j] = m_new


def _attn_a_kernel(nslope_ref, lam_ref, q_ref, k_ref, v_ref, mk_ref, mv_ref, sg_ref, oin_ref,
                   o_ref, m_ref, l_ref, acc_ref, *, tq, tk, nk, q_is_meta, out_scale):
    del oin_ref
    nslope = nslope_ref[pl.program_id(1)]
    q = q_ref[...]
    qs = (q[:, :DA_DIM], q[:, DA_DIM:])
    row = lax.broadcasted_iota(jnp.int32, (tq, 1), 0)
    if q_is_meta:
        qpos = row.astype(F32)
    else:
        qpos = (row + (pl.program_id(2) * tq + N_META)).astype(F32)
    m_ref[...] = jnp.full(m_ref.shape, NEG_INF, F32)
    l_ref[...] = jnp.zeros(l_ref.shape, F32)
    acc_ref[...] = jnp.zeros(acc_ref.shape, F32)

    def tile(kb, vb, kpos):
        bias = jnp.abs(qpos - kpos) * nslope
        for j in range(2):
            s = _dot_nt(qs[j], kb[:, j * KX_W:j * KX_W + DA_DIM]) + bias
            _softmax_step(s, vb, m_ref, l_ref, acc_ref, j)

    def body(i, carry):
        k0 = pl.multiple_of(i * tk, tk)
        col = lax.broadcasted_iota(jnp.int32, (1, tk), 1)
        tile(k_ref[pl.ds(k0, tk), :], v_ref[pl.ds(k0, tk), :], (col + (k0 + N_META)).astype(F32))
        return carry

    lax.fori_loop(0, nk, body, 0)
    tile(mk_ref[...], mv_ref[...], lax.broadcasted_iota(jnp.int32, (1, N_META), 1).astype(F32))
    o = acc_ref[0] / l_ref[0] - lam_ref[0] * (acc_ref[1] / l_ref[1])
    o_ref[...] = (_rms(o, sg_ref[...]) * out_scale).astype(o_ref.dtype)


def _attn_b_kernel(q_ref, k_ref, v_ref, mk_ref, mv_ref, oin_ref,
                   o_ref, qs_ref, m_ref, l_ref, acc_ref, *, tq, tk, nk):
    del oin_ref
    for g in range(GB_GROUP):
        qs_ref[g * tq:(g + 1) * tq, :] = q_ref[:, g * GB_DIM:(g + 1) * GB_DIM]
    qs = qs_ref[...]
    m_ref[...] = jnp.full(m_ref.shape, NEG_INF, F32)
    l_ref[...] = jnp.zeros(l_ref.shape, F32)
    acc_ref[...] = jnp.zeros(acc_ref.shape, F32)

    def tile(kb, vb):
        _softmax_step(_dot_nt(qs, kb), vb, m_ref, l_ref, acc_ref, 0)

    def body(i, carry):
        k0 = pl.multiple_of(i * tk, tk)
        tile(k_ref[pl.ds(k0, tk), :], v_ref[pl.ds(k0, tk), :])
        return carry

    lax.fori_loop(0, nk, body, 0)
    tile(mk_ref[...], mv_ref[...])
    o = acc_ref[0] / l_ref[0]
    for g in range(GB_GROUP):
        o_ref[:, g * GB_DIM:(g + 1) * GB_DIM] = o[g * tq:(g + 1) * tq].astype(o_ref.dtype)


def _attention_meta(o_buf, aq, akx, av, bq, bk, bv, nslope, lam, subln_g, out_scale, *,
                    n_seq, seq_len, row0, seq0, n_real):
    assert row0 % seq_len == 0 and seq_len % META_TK == 0
    nk = seq_len // META_TK
    kv_blk0 = row0 // seq_len
    meta_blk0 = n_real // N_META + seq0
    any_spec = pl.BlockSpec(memory_space=pl.ANY)
    smem = pl.BlockSpec(memory_space=pltpu.SMEM)

    tq = N_META
    nq = 1
    q_map = lambda b, h, i: (meta_blk0 + b, h)
    kv_map = lambda b, h, i: (kv_blk0 + b, h)
    mkv_map = lambda b, h, i: (meta_blk0 + b, h)
    w = 2 * DA_DIM
    o_buf = pl.pallas_call(
        functools.partial(_attn_a_kernel, tq=tq, tk=META_TK, nk=nk, q_is_meta=True, out_scale=out_scale),
        grid=(n_seq, DA_HEADS, nq),
        in_specs=[
            smem, smem,
            pl.BlockSpec((tq, w), q_map),
            pl.BlockSpec((seq_len, 2 * KX_W), kv_map),
            pl.BlockSpec((seq_len, w), kv_map),
            pl.BlockSpec((N_META, 2 * KX_W), mkv_map),
            pl.BlockSpec((N_META, w), mkv_map),
            pl.BlockSpec((1, w), lambda b, h, i: (0, 0)),
            any_spec,
        ],
        out_specs=pl.BlockSpec((tq, w), q_map),
        out_shape=jax.ShapeDtypeStruct(o_buf.shape, o_buf.dtype),
        scratch_shapes=[
            pltpu.VMEM((2, tq, 1), F32),
            pltpu.VMEM((2, tq, 1), F32),
            pltpu.VMEM((2, tq, w), F32),
        ],
        input_output_aliases={8: 0},
        compiler_params=_cparams(("parallel", "parallel", "parallel")),
        name="attn_diff_meta",
    )(nslope, lam, aq, akx, av, akx, av, subln_g, o_buf)

    qw = GB_GROUP * GB_DIM
    ocol0 = DA_WIDTH // qw
    q_map = lambda b, c, i: (meta_blk0 + b, c)
    o_map = lambda b, c, i: (meta_blk0 + b, ocol0 + c)
    kv_map = lambda b, c, i: (c, kv_blk0 + b, 0)
    mkv_map = lambda b, c, i: (c, meta_blk0 + b, 0)
    o_buf = pl.pallas_call(
        functools.partial(_attn_b_kernel, tq=tq, tk=META_TK, nk=nk),
        grid=(n_seq, GB_KV, nq),
        in_specs=[
            pl.BlockSpec((tq, qw), q_map),
            pl.BlockSpec((None, seq_len, GB_DIM), kv_map),
            pl.BlockSpec((None, seq_len, GB_DIM), kv_map),
            pl.BlockSpec((None, N_META, GB_DIM), mkv_map),
            pl.BlockSpec((None, N_META, GB_DIM), mkv_map),
            any_spec,
        ],
        out_specs=pl.BlockSpec((tq, qw), o_map),
        out_shape=jax.ShapeDtypeStruct(o_buf.shape, o_buf.dtype),
        scratch_shapes=[
            pltpu.VMEM((GB_GROUP * tq, GB_DIM), BF16),
            pltpu.VMEM((1, GB_GROUP * tq, 1), F32),
            pltpu.VMEM((1, GB_GROUP * tq, 1), F32),
            pltpu.VMEM((1, GB_GROUP * tq, GB_DIM), F32),
        ],
        input_output_aliases={5: 0},
        compiler_params=_cparams(("parallel", "parallel", "parallel")),
        name="attn_gqa_meta",
    )(bq, bk, bv, bk, bv, o_buf)
    return o_buf


def _own_meta_rows(shape, seq):
    r = lax.broadcasted_iota(jnp.int32, shape, 0)
    return (r >= seq * N_META) & (r < (seq + 1) * N_META)


def _split3(x):
    hi = x.astype(BF16).astype(F32)
    r = x - hi
    mid = r.astype(BF16).astype(F32)
    return hi, mid, (r - mid).astype(BF16).astype(F32)


def _attn_a_t_kernel(nslope_ref, lam_ref, q_ref, k_ref, vt_ref, mk_ref, mvt_ref, sg_ref, oin_ref,
                     o_ref, qx_ref, s_ref, cmax_ref, p_ref, alpha_ref, m_ref, acc_ref, *,
                     t, nk, seq0, out_scale):
    del oin_ref
    qi = pl.program_id(2)
    nslope = nslope_ref[pl.program_id(1)]
    q = q_ref[...]
    qpos = (lax.broadcasted_iota(jnp.int32, (t, 1), 0) + (qi * t + N_META)).astype(F32)
    lane = lax.broadcasted_iota(jnp.int32, (t, POS_W), 1)
    c1, c2, c3 = _split3(jnp.full((t, POS_W), -nslope, F32))
    b1, b2, b3 = _split3(jnp.broadcast_to(nslope * qpos, (t, POS_W)))
    coef = jnp.where((lane == 0) | (lane == 3), c1, jnp.where((lane == 1) | (lane == 4), c2,
           jnp.where((lane == 2) | (lane == 5), c3, jnp.where(lane == 6, b1, jnp.where(lane == 7, b2,
           jnp.where(lane == 8, b3, 0.0))))))
    for v, sign in enumerate((1.0, 0.0, -1.0)):
        ext = (coef * sign).astype(BF16)
        for j in range(2):
            qx_ref[2 * v + j] = jnp.concatenate([q[:, j * DA_DIM:(j + 1) * DA_DIM], ext], axis=1)
    m_ref[...] = jnp.full(m_ref.shape, NEG_INF, F32)
    acc_ref[...] = jnp.zeros(acc_ref.shape, F32)

    def scores(kb, v, slot, bias=None, mask=None):
        for j in range(2):
            s = _dot_nt(kb[:, j * KX_W:(j + 1) * KX_W], qx_ref[2 * v + j])
            if bias is not None:
                s = s + bias
            if mask is not None:
                s = jnp.where(mask, s, NEG_INF)
            s_ref[slot, j] = s
            cmax_ref[slot, j] = jnp.max(s, axis=0, keepdims=True)

    def softmax(slot):
        for j in range(2):
            m_prev = m_ref[j]
            m_new = jnp.maximum(m_prev, cmax_ref[slot, j])
            alpha = jnp.exp2(m_prev - m_new)
            p = jnp.exp2(s_ref[slot, j] - m_new)
            m_ref[j] = m_new
            alpha_ref[slot, j] = alpha
            p_ref[slot, j] = p.astype(BF16)

    def accumulate(vtb, slot):
        vtx = jnp.concatenate([vtb, jnp.ones((ONES_ROWS, vtb.shape[1]), BF16)], axis=0)
        for j in range(2):
            acc_ref[j] = alpha_ref[slot, j] * acc_ref[j] + _dot(vtx, p_ref[slot, j])

    def other(n):
        return jnp.where(n >= qi, n + 1, n)

    def other_scores(n, slot):
        i = other(n)
        scores(k_ref[pl.ds(pl.multiple_of(i * t, t), t), :], jnp.where(i < qi, 0, 2), slot)

    kk = lax.broadcasted_iota(jnp.int32, (t, t), 0)
    qq = lax.broadcasted_iota(jnp.int32, (t, t), 1)
    scores(k_ref[pl.ds(pl.multiple_of(qi * t, t), t), :], 1, 0, bias=jnp.abs(kk - qq).astype(F32) * nslope)
    other_scores(0, 1)
    softmax(0)

    def pair(jj, carry):
        p = 2 * jj + 1
        accumulate(vt_ref[jnp.where(p == 1, qi, other(p - 2))], 0)
        other_scores(p, 0)
        softmax(1)
        accumulate(vt_ref[other(p - 1)], 1)
        other_scores(p + 1, 1)
        softmax(0)
        return carry

    lax.fori_loop(0, (nk - 2) // 2, pair, 0, unroll=5)
    accumulate(vt_ref[qi if nk == 2 else other(nk - 3)], 0)
    scores(mk_ref[...], 0, 0, mask=_own_meta_rows((t, t), seq0 + pl.program_id(0)))
    softmax(1)
    accumulate(vt_ref[other(nk - 2)], 1)
    softmax(0)
    accumulate(mvt_ref[...], 0)

    dv = 2 * DA_DIM
    a0, a1 = acc_ref[0], acc_ref[1]
    ot = a0[:dv] / a0[dv:dv + 1] - lam_ref[0] * (a1[:dv] / a1[dv:dv + 1])
    o_ref[...] = (_rms(ot.T, sg_ref[...]) * out_scale).astype(o_ref.dtype)


def _attn_b_t_kernel(q_ref, k_ref, vt_ref, mk_ref, mvt_ref, oin_ref,
                     o_ref, qs_ref, s0_ref, s1_ref, c0_ref, c1_ref, p0_ref, p1_ref, a0_ref, a1_ref,
                     m_ref, acc_ref, *, tq, t, nk, seq0):
    del oin_ref
    assert nk % 2 == 0
    s_ref, cmax_ref, p_ref, alpha_ref = (s0_ref, s1_ref), (c0_ref, c1_ref), (p0_ref, p1_ref), (a0_ref, a1_ref)
    for g in range(GB_GROUP):
        qs_ref[g * tq:(g + 1) * tq, :] = q_ref[:, g * GB_DIM:(g + 1) * GB_DIM]
    qs = qs_ref[...]
    nq = GB_GROUP * tq
    m_ref[...] = jnp.full(m_ref.shape, NEG_INF, F32)
    acc_ref[...] = jnp.zeros(acc_ref.shape, F32)

    def scores(kb, slot, mask=None):
        s = _dot_nt(kb, qs)
        if mask is not None:
            s = jnp.where(mask, s, NEG_INF)
        s_ref[slot][...] = s
        cmax_ref[slot][...] = jnp.max(s, axis=0, keepdims=True)

    def seq_scores(i, slot):
        scores(k_ref[pl.ds(pl.multiple_of(i * t, t), t), :], slot)

    def softmax(slot):
        m_prev = m_ref[...]
        m_new = jnp.maximum(m_prev, cmax_ref[slot][...])
        alpha = jnp.exp2(m_prev - m_new)
        p = jnp.exp2(s_ref[slot][...] - m_new)
        m_ref[...] = m_new
        alpha_ref[slot][...] = alpha
        p_ref[slot][...] = p.astype(BF16)

    def accumulate(vtb, slot):
        vtx = jnp.concatenate([vtb, jnp.ones((ONES_ROWS, vtb.shape[1]), BF16)], axis=0)
        acc_ref[...] = alpha_ref[slot][...] * acc_ref[...] + _dot(vtx, p_ref[slot][...])

    seq_scores(0, 0)
    seq_scores(1, 1)
    softmax(0)

    def pair(jj, carry):
        p = 2 * jj + 1
        accumulate(vt_ref[p - 1], 0)
        seq_scores(p + 1, 0)
        softmax(1)
        accumulate(vt_ref[p], 1)
        seq_scores(p + 2, 1)
        softmax(0)
        return carry

    lax.fori_loop(0, (nk - 2) // 2, pair, 0, unroll=5)
    accumulate(vt_ref[nk - 2], 0)
    scores(mk_ref[...], 0, mask=_own_meta_rows((t, nq), seq0 + pl.program_id(0)))
    softmax(1)
    accumulate(vt_ref[nk - 1], 1)
    softmax(0)
    accumulate(mvt_ref[...], 0)
    acc = acc_ref[...]
    ot = acc[:GB_DIM] / acc[GB_DIM:GB_DIM + 1]
    for g in range(0, GB_GROUP, 2):
        pair = jnp.concatenate([ot[:, g * tq:(g + 1) * tq], ot[:, (g + 1) * tq:(g + 2) * tq]], axis=0)
        o_ref[:, g * GB_DIM:(g + 2) * GB_DIM] = pair.T.astype(o_ref.dtype)


def _attention_main(o_buf, aq, akx, avt, bq, bk, bvt, nslope, lam, subln_g, out_scale, *,
                    n_seq, seq_len, row0, seq0, n_real):
    t = ATTN_TILE
    assert row0 % seq_len == 0 and seq_len % t == 0 and n_real % t == 0
    nk = seq_len // t
    kv_blk0 = row0 // seq_len
    meta_tile = n_real // t
    any_spec = pl.BlockSpec(memory_space=pl.ANY)
    smem = pl.BlockSpec(memory_space=pltpu.SMEM)

    q_blk0 = row0 // t
    q_map = lambda b, h, i: (q_blk0 + b * nk + i, h)
    w = 2 * DA_DIM
    o_buf = pl.pallas_call(
        functools.partial(_attn_a_t_kernel, t=t, nk=nk, seq0=seq0, out_scale=out_scale),
        grid=(n_seq, DA_HEADS, nk),
        in_specs=[
            smem, smem,
            pl.BlockSpec((t, w), q_map),
            pl.BlockSpec((seq_len, 2 * KX_W), lambda b, h, i: (kv_blk0 + b, h)),
            pl.BlockSpec((nk, w, t), lambda b, h, i: (kv_blk0 + b, h, 0)),
            pl.BlockSpec((t, 2 * KX_W), lambda b, h, i: (meta_tile, h)),
            pl.BlockSpec((None, w, t), lambda b, h, i: (meta_tile, h, 0)),
            pl.BlockSpec((1, w), lambda b, h, i: (0, 0)),
            any_spec,
        ],
        out_specs=pl.BlockSpec((t, w), q_map),
        out_shape=jax.ShapeDtypeStruct(o_buf.shape, o_buf.dtype),
        scratch_shapes=[
            pltpu.VMEM((6, t, KX_W), BF16),
            pltpu.VMEM((2, 2, t, t), F32),
            pltpu.VMEM((2, 2, 1, t), F32),
            pltpu.VMEM((2, 2, t, t), BF16),
            pltpu.VMEM((2, 2, 1, t), F32),
            pltpu.VMEM((2, 1, t), F32),
            pltpu.VMEM((2, w + ONES_ROWS, t), F32),
        ],
        input_output_aliases={8: 0},
        compiler_params=_cparams(("parallel", "parallel", "parallel")),
        name="attn_diff",
    )(nslope, lam, aq, akx, avt, akx, avt, subln_g, o_buf)

    tq = ATTN_B_TQ
    nq = seq_len // tq
    qw = GB_GROUP * GB_DIM
    ocol0 = DA_WIDTH // qw
    q_blk0 = row0 // tq
    o_buf = pl.pallas_call(
        functools.partial(_attn_b_t_kernel, tq=tq, t=t, nk=nk, seq0=seq0),
        grid=(n_seq, GB_KV, nq),
        in_specs=[
            pl.BlockSpec((tq, qw), lambda b, c, i: (q_blk0 + b * nq + i, c)),
            pl.BlockSpec((None, seq_len, GB_DIM), lambda b, c, i: (c, kv_blk0 + b, 0)),
            pl.BlockSpec((nk, GB_DIM, t), lambda b, c, i: (kv_blk0 + b, c, 0)),
            pl.BlockSpec((None, t, GB_DIM), lambda b, c, i: (c, meta_tile, 0)),
            pl.BlockSpec((None, GB_DIM, t), lambda b, c, i: (meta_tile, c, 0)),
            any_spec,
        ],
        out_specs=pl.BlockSpec((tq, qw), lambda b, c, i: (q_blk0 + b * nq + i, ocol0 + c)),
        out_shape=jax.ShapeDtypeStruct(o_buf.shape, o_buf.dtype),
        scratch_shapes=[
            pltpu.VMEM((GB_GROUP * tq, GB_DIM), BF16),
            pltpu.VMEM((t, GB_GROUP * tq), F32), pltpu.VMEM((t, GB_GROUP * tq), F32),
            pltpu.VMEM((1, GB_GROUP * tq), F32), pltpu.VMEM((1, GB_GROUP * tq), F32),
            pltpu.VMEM((t, GB_GROUP * tq), BF16), pltpu.VMEM((t, GB_GROUP * tq), BF16),
            pltpu.VMEM((1, GB_GROUP * tq), F32), pltpu.VMEM((1, GB_GROUP * tq), F32),
            pltpu.VMEM((1, GB_GROUP * tq), F32),
            pltpu.VMEM((GB_DIM + ONES_ROWS, GB_GROUP * tq), F32),
        ],
        input_output_aliases={5: 0},
        compiler_params=_cparams(("parallel", "parallel", "parallel")),
        name="attn_gqa",
    )(bq, bk, bvt, bk, bvt, o_buf)
    return o_buf


def _swiglu(u, wg, wu, wd):
    a = _dot(u, wg)
    b = _dot(u, wu)
    hid = (a * jax.nn.sigmoid(a) * b).astype(BF16)
    return _dot(hid, wd)


def _out_ffn_kernel(o_ref, h_ref, wo_ref, g_ref, wg_ref, wu_ref, wd_ref, out_ref):
    h1 = h_ref[...] + _dot(o_ref[...], wo_ref[...])
    u = _rms(h1, g_ref[...]).astype(BF16)
    out_ref[...] = h1 + _swiglu(u, wg_ref[...], wu_ref[...], wd_ref[...])


def _out_ffn(o, h, wo, g, wg, wu, wd):
    rows, d = h.shape
    dff = wg.shape[1]
    tm = ROW_TILE
    row = lambda i: (i, 0)
    return pl.pallas_call(
        _out_ffn_kernel,
        grid=(rows // tm,),
        in_specs=[
            pl.BlockSpec((tm, d), row),
            pl.BlockSpec((tm, d), row),
            _const_spec((d, d)),
            _const_spec((1, d)),
            _const_spec((d, dff)),
            _const_spec((d, dff)),
            _const_spec((dff, d)),
        ],
        out_specs=pl.BlockSpec((tm, d), row),
        out_shape=jax.ShapeDtypeStruct((rows, d), F32),
        compiler_params=_cparams(("parallel",)),
        name="out_proj_ffn",
    )(o, h, wo, g, wg, wu, wd)


def _out_router_kernel(o_ref, h_ref, wo_ref, g_ref, wrh_ref, wrl_ref, h1_ref, u_ref, ri_ref):
    h1 = h_ref[...] + _dot(o_ref[...], wo_ref[...])
    h1_ref[...] = h1
    u = _rms(h1, g_ref[...])
    u_ref[...] = u
    uh = u.astype(BF16)
    ul = (u - uh.astype(F32)).astype(BF16)
    wrh = wrh_ref[...]
    logits = _dot(uh, wrh) + _dot(ul, wrh) + _dot(uh, wrl_ref[...])
    lane = lax.broadcasted_iota(jnp.int32, logits.shape, 1)
    lg = jnp.where(lane < N_EXPERTS, logits, NEG_INF)
    v1 = jnp.max(lg, axis=1, keepdims=True)
    i1 = jnp.min(jnp.where(lg == v1, lane, LANES), axis=1, keepdims=True)
    lg2 = jnp.where(lane == i1, NEG_INF, lg)
    v2 = jnp.max(lg2, axis=1, keepdims=True)
    i2 = jnp.min(jnp.where(lg2 == v2, lane, LANES), axis=1, keepdims=True)
    e = jnp.exp(v2 - v1)
    g1 = 1.0 / (1.0 + e)
    g2 = e * g1
    ri_ref[...] = jnp.where(lane == 0, i1.astype(F32),
                            jnp.where(lane == 1, i2.astype(F32),
                                      jnp.where(lane == 2, g1, jnp.where(lane == 3, g2, 0.0))))


def _out_router(o, h, wo, g, wrh, wrl, n_rows):
    d = h.shape[1]
    tm = ROW_TILE
    row = lambda i: (i, 0)
    return pl.pallas_call(
        _out_router_kernel,
        grid=(n_rows // tm,),
        in_specs=[
            pl.BlockSpec((tm, d), row),
            pl.BlockSpec((tm, d), row),
            _const_spec((d, d)),
            _const_spec((1, d)),
            _const_spec((d, LANES)),
            _const_spec((d, LANES)),
        ],
        out_specs=[
            pl.BlockSpec((tm, d), row),
            pl.BlockSpec((tm, d), row),
            pl.BlockSpec((tm, LANES), row),
        ],
        out_shape=[
            jax.ShapeDtypeStruct((n_rows, d), F32),
            jax.ShapeDtypeStruct((n_rows, d), F32),
            jax.ShapeDtypeStruct((n_rows, LANES), F32),
        ],
        compiler_params=_cparams(("parallel",)),
        name="out_proj_router",
    )(o, h, wo, g, wrh, wrl)


def _dispatch_kernel(slot_ref, u_ref, xs_in, xs_hbm, sem, *, tm):
    del xs_in

    def copy(t, k):
        return pltpu.make_async_copy(u_ref.at[pl.ds(t, 1)], xs_hbm.at[pl.ds(slot_ref[0, k, t], 1)], sem)

    def issue(t, carry):
        for k in range(TOP_K):
            copy(t, k).start()
        return carry

    lax.fori_loop(0, tm, issue, 0, unroll=8)
    for k in range(TOP_K):
        pltpu.make_async_copy(u_ref, xs_hbm.at[pl.ds(0, tm)], sem).wait()


def _dispatch(slots, u, xs_init, tm):
    n_rows, d = u.shape
    return pl.pallas_call(
        functools.partial(_dispatch_kernel, tm=tm),
        grid=(n_rows // tm,),
        in_specs=[
            pl.BlockSpec((1, TOP_K, tm), lambda i: (i, 0, 0), memory_space=pltpu.SMEM),
            pl.BlockSpec((tm, d), lambda i: (i, 0)),
            pl.BlockSpec(memory_space=pl.ANY),
        ],
        out_specs=pl.BlockSpec(memory_space=pl.ANY),
        out_shape=jax.ShapeDtypeStruct(xs_init.shape, xs_init.dtype),
        scratch_shapes=[pltpu.SemaphoreType.DMA(())],
        input_output_aliases={2: 0},
        compiler_params=pltpu.CompilerParams(dimension_semantics=("arbitrary",), has_side_effects=True,
                                             vmem_limit_bytes=VMEM_LIMIT),
        name="moe_dispatch",
    )(slots, u, xs_init)


def _moe_ffn_kernel(te_ref, nu_ref, x_ref, wg_ref, wu_ref, wd_ref, y_ref):
    del te_ref
    used = pl.program_id(0) < nu_ref[0]

    @pl.when(used)
    def _():
        y_ref[...] = _swiglu(x_ref[...].astype(BF16), wg_ref[0], wu_ref[0], wd_ref[0])

    @pl.when(jnp.logical_not(used))
    def _():
        y_ref[...] = jnp.zeros(y_ref.shape, y_ref.dtype)


def _moe_ffn(tile_expert, n_used, xs, wg, wu, wd, tm):
    n_slots, d = xs.shape
    dff = wg.shape[2]
    x_map = lambda i, te, nu: (jnp.minimum(i, nu[0] - 1), 0)
    w_map = lambda i, te, nu: (te[i], 0, 0)
    grid_spec = pltpu.PrefetchScalarGridSpec(
        num_scalar_prefetch=2,
        grid=(n_slots // tm,),
        in_specs=[
            pl.BlockSpec((tm, d), x_map),
            pl.BlockSpec((1, d, dff), w_map, pipeline_mode=pl.Buffered(1)),
            pl.BlockSpec((1, d, dff), w_map, pipeline_mode=pl.Buffered(1)),
            pl.BlockSpec((1, dff, d), w_map, pipeline_mode=pl.Buffered(1)),
        ],
        out_specs=pl.BlockSpec((tm, d), lambda i, te, nu: (i, 0)),
    )
    return pl.pallas_call(
        _moe_ffn_kernel,
        grid_spec=grid_spec,
        out_shape=jax.ShapeDtypeStruct((n_slots, d), F32),
        compiler_params=_cparams(("arbitrary",)),
        name="moe_ffn",
    )(tile_expert, n_used, xs, wg, wu, wd)


def _combine_kernel(slot_ref, h1_ref, ri_ref, g_ref, ys_hbm, out_ref, buf_ref, sem, *, tm, final_norm):
    def copy(t, k):
        return pltpu.make_async_copy(ys_hbm.at[pl.ds(slot_ref[0, k, t], 1)], buf_ref.at[k, pl.ds(t, 1)], sem)

    def issue(t, carry):
        for k in range(TOP_K):
            copy(t, k).start()
        return carry

    lax.fori_loop(0, tm, issue, 0, unroll=8)
    for k in range(TOP_K):
        pltpu.make_async_copy(ys_hbm.at[pl.ds(0, tm)], buf_ref.at[k], sem).wait()
    ri = ri_ref[...]
    h2 = h1_ref[...] + ri[:, 2:3] * buf_ref[0] + ri[:, 3:4] * buf_ref[1]
    out_ref[...] = _rms(h2, g_ref[...]) if final_norm else h2


def _combine(slots, h1, ri, g, ys, tm, final_norm):
    n_rows, d = h1.shape
    row = lambda i: (i, 0)
    return pl.pallas_call(
        functools.partial(_combine_kernel, tm=tm, final_norm=final_norm),
        grid=(n_rows // tm,),
        in_specs=[
            pl.BlockSpec((1, TOP_K, tm), lambda i: (i, 0, 0), memory_space=pltpu.SMEM),
            pl.BlockSpec((tm, d), row),
            pl.BlockSpec((tm, LANES), row),
            _const_spec((1, d)),
            pl.BlockSpec(memory_space=pl.ANY),
        ],
        out_specs=pl.BlockSpec((tm, d), row),
        out_shape=jax.ShapeDtypeStruct((n_rows, d), F32),
        scratch_shapes=[pltpu.VMEM((TOP_K, tm, d), F32), pltpu.SemaphoreType.DMA(())],
        compiler_params=_cparams(("arbitrary",)),
        name="moe_combine_norm",
    )(slots, h1, ri, g, ys)


def _final_norm_kernel(h_ref, g_ref, out_ref):
    out_ref[...] = _rms(h_ref[...], g_ref[...])


def _final_norm(h, g, n_rows):
    d = h.shape[1]
    tm = ROW_TILE
    row = lambda i: (i, 0)
    return pl.pallas_call(
        _final_norm_kernel,
        grid=(n_rows // tm,),
        in_specs=[pl.BlockSpec((tm, d), row), _const_spec((1, d))],
        out_specs=pl.BlockSpec((tm, d), row),
        out_shape=jax.ShapeDtypeStruct((n_rows, d), F32),
        compiler_params=_cparams(("parallel",)),
        name="final_norm",
    )(h, g)


def _rope_tables(groups, pad_rows):
    n_freq = GB_DIM // 4
    inv = ROPE_THETA ** (-jnp.arange(n_freq, dtype=F32) * 2.0 / (GB_DIM // 2))
    rows, cols = [], []
    for n_seq, seq_len in groups:
        t = jnp.arange(seq_len, dtype=jnp.int32)
        rows.append(jnp.tile((t // GRID_W).astype(F32), n_seq))
        cols.append(jnp.tile((t % GRID_W).astype(F32), n_seq))
    n_total_seq = sum(n for n, _ in groups)
    meta = jnp.tile(jnp.arange(N_META, dtype=F32) - N_META, n_total_seq)
    pad = jnp.zeros((pad_rows,), F32)
    r = jnp.concatenate(rows + [meta, pad])
    c = jnp.concatenate(cols + [meta, pad])
    ang = jnp.concatenate([r[:, None] * inv, c[:, None] * inv], axis=-1)
    cos, sin = jnp.cos(ang), jnp.sin(ang)
    reps = LANES // GB_DIM
    return jnp.tile(jnp.concatenate([cos, cos], axis=-1), (1, reps)), jnp.tile(jnp.concatenate([-sin, sin], axis=-1), (1, reps))


def _key_position_columns(groups, pad_rows):
    n_total_seq = sum(n for n, _ in groups)
    pos = jnp.concatenate(
        [jnp.tile(jnp.arange(s, dtype=jnp.int32) + N_META, n) for n, s in groups]
        + [jnp.tile(jnp.arange(N_META, dtype=jnp.int32), n_total_seq), jnp.zeros((pad_rows,), jnp.int32)])
    hi = ((pos // LANES) * LANES).astype(F32)[:, None]
    lo = (pos % LANES).astype(F32)[:, None]
    col = jnp.arange(POS_W)[None, :]
    table = jnp.where(col < 3, hi, jnp.where(col < 6, lo, jnp.where(col < 9, 1.0, 0.0)))
    return table.astype(BF16)


def _half_split_perm():
    return jnp.concatenate([jnp.arange(0, GB_DIM, 2), jnp.arange(1, GB_DIM, 2)])


def _route(ri, tm):
    n = ri.shape[0]
    experts = ri[:, :TOP_K].astype(jnp.int32).T.reshape(-1)
    onehot = (experts[:, None] == jnp.arange(N_EXPERTS, dtype=jnp.int32)[None, :]).astype(jnp.int32)
    incl = jnp.cumsum(onehot, axis=0)
    counts = incl[-1]
    pos = jnp.sum((incl - onehot) * onehot, axis=1)
    padded = ((counts + tm - 1) // tm) * tm
    ends = jnp.cumsum(padded)
    starts = ends - padded
    slot = jnp.sum(onehot * starts[None, :], axis=1) + pos
    slots = slot.reshape(TOP_K, n // tm, tm).transpose(1, 0, 2)
    n_slots = TOP_K * n + N_EXPERTS * tm
    tile_start = jnp.arange(n_slots // tm, dtype=jnp.int32) * tm
    tile_expert = jnp.sum((tile_start[:, None] >= ends[None, :]).astype(jnp.int32), axis=1)
    n_used = (ends[-1] // tm).astype(jnp.int32)
    last_used_expert = jnp.sum((jnp.maximum(ends[-1] - tm, 0) >= ends).astype(jnp.int32))
    tile_expert = jnp.where(tile_start < ends[-1], tile_expert, last_used_expert).astype(jnp.int32)
    return slots.astype(jnp.int32), tile_expert, n_used.reshape(1), n_slots


def kernel(x_prompt, x_sample, meta_tokens, norm1_g, w_in, lambda_q1, lambda_k1, lambda_q2, lambda_k2, subln_g, q_norm_g, k_norm_g, w_out, norm2_g, ffn_w_gate, ffn_w_up, ffn_w_down, router_w, moe_w_gate, moe_w_up, moe_w_down, final_norm_g):
    depth = w_in.shape[0]
    d = x_prompt.shape[-1]
    groups = [(x_prompt.shape[0], x_prompt.shape[1]), (x_sample.shape[0], x_sample.shape[1])]
    n_seq_total = sum(n for n, _ in groups)
    n_real = sum(n * s for n, s in groups)
    assert n_real % ROW_TILE == 0
    meta_rows = n_seq_total * N_META
    meta_pad = -(-meta_rows // ROW_TILE) * ROW_TILE
    n_rows = n_real + meta_pad

    h = jnp.concatenate([
        x_prompt.reshape(-1, d), x_sample.reshape(-1, d),
        jnp.tile(meta_tokens.astype(F32), (n_seq_total, 1)),
        jnp.zeros((meta_pad - meta_rows, d), F32)], axis=0)

    cs, sn = _rope_tables(groups, meta_pad - meta_rows)
    kpos = _key_position_columns(groups, meta_pad - meta_rows)
    perm = _half_split_perm()
    c0 = 3 * DA_WIDTH
    col_perm = jnp.concatenate(
        [jnp.arange(c0)]
        + [c0 + hd * GB_DIM + perm for hd in range(GB_HEADS + GB_KV)]
        + [jnp.arange(c0 + QK_NORM_WIDTH, w_in.shape[2])])
    head_id = jnp.arange(QK_NORM_WIDTH) // GB_DIM
    block_diag = (head_id[:, None] == head_id[None, :]).astype(BF16)
    slopes = (2.0 ** (-8.0 / DA_HEADS)) ** jnp.arange(1, DA_HEADS + 1, dtype=F32)
    nslope = -slopes * math.log2(math.e)

    out = None
    for layer in range(depth):
        last = layer == depth - 1
        lambda_init = 0.8 - 0.6 * math.exp(-0.3 * layer)
        lam = (jnp.exp(jnp.sum(lambda_q1[layer].astype(F32) * lambda_k1[layer].astype(F32)))
               - jnp.exp(jnp.sum(lambda_q2[layer].astype(F32) * lambda_k2[layer].astype(F32))) + lambda_init)
        gqk = jnp.concatenate([jnp.tile(q_norm_g[layer][perm], GB_HEADS), jnp.tile(k_norm_g[layer][perm], GB_KV)])
        w_l = w_in[layer]
        wvt = jnp.concatenate([w_l[:, 2 * DA_WIDTH:3 * DA_WIDTH], w_l[:, c0 + QK_NORM_WIDTH:]], axis=1).T
        aq, akx, av, avt, bq, bk, bv, bvt = _in_proj(
            h, norm1_g[layer][None], w_l[:, col_perm].astype(BF16), wvt.astype(BF16), cs, sn, kpos,
            gqk[None].astype(F32), block_diag)

        o = jnp.zeros((n_rows, DA_WIDTH + GB_WIDTH), BF16)
        row0, seq0 = 0, 0
        for n_seq, seq_len in groups:
            where = dict(n_seq=n_seq, seq_len=seq_len, row0=row0, seq0=seq0, n_real=n_real)
            sg = subln_g[layer][None].astype(F32)
            o = _attention_main(o, aq, akx, avt, bq, bk, bvt, nslope, lam.reshape(1), sg, 1.0 - lambda_init, **where)
            if not last:
                o = _attention_meta(o, aq, akx, av, bq, bk, bv, nslope, lam.reshape(1), sg, 1.0 - lambda_init, **where)
            row0 += n_seq * seq_len
            seq0 += n_seq

        wo = w_out[layer].astype(BF16)
        g2 = norm2_g[layer][None]
        if layer % 2 == 0:
            i = layer // 2
            if last:
                h = _out_ffn(o[:n_real], h[:n_real], wo, g2, ffn_w_gate[i].astype(BF16), ffn_w_up[i].astype(BF16),
                             ffn_w_down[i].astype(BF16))
                out = _final_norm(h, final_norm_g[None], n_real)
            else:
                h = _out_ffn(o, h, wo, g2, ffn_w_gate[i].astype(BF16), ffn_w_up[i].astype(BF16),
                             ffn_w_down[i].astype(BF16))
        else:
            i = layer // 2
            n_tok = n_real if last else n_rows
            wr = jnp.zeros((d, LANES), F32).at[:, :N_EXPERTS].set(router_w[i])
            wrh = wr.astype(BF16)
            wrl = (wr - wrh.astype(F32)).astype(BF16)
            h1, u, ri = _out_router(o, h, wo, g2, wrh, wrl, n_tok)
            tm = ROW_TILE
            slots, tile_expert, n_used, n_slots = _route(ri, tm)
            xs = _dispatch(slots, u, jnp.zeros((n_slots, d), F32), tm)
            ys = _moe_ffn(tile_expert, n_used, xs, moe_w_gate[i].astype(BF16), moe_w_up[i].astype(BF16),
                          moe_w_down[i].astype(BF16), tm)
            res = _combine(slots, h1, ri, final_norm_g[None], ys, tm, final_norm=last)
            if last:
                out = res
            else:
                h = res

    y_prompt = out[:groups[0][0] * groups[0][1]].reshape(x_prompt.shape)
    y_sample = out[groups[0][0] * groups[0][1]:n_real].reshape(x_sample.shape)
    return (y_prompt, y_sample)
```
